```python
import jax, jax.numpy as jnp
from jax import lax
import numpy as np

D_MODEL = 4096
BATCH = 8
SEQ = 2048
DEPTH = 4

CTX_LEN = 256
GRID_W = 64
N_MOD = 9
MOD_RANK = 256
D_FF = 3072
CONV_W = 1024
CONV_K = 3
GLA_HEADS = 4
GLA_DK = 192
GLA_DV = 384
GLA_DK_T = GLA_HEADS * GLA_DK
GLA_DV_T = GLA_HEADS * GLA_DV
GLA_DECAY_RANK = 16
GLA_TAU = 16.0
GLA_CHUNK = 64
NA_HEADS = 12
NA_HD = 128
NA_W = NA_HEADS * NA_HD
NA_KR = 8
NA_KC = 16
ROPE_THETA = 10000.0
EPS = 1e-6
IN_SIZES = (CONV_W, CONV_W, CONV_W,
            GLA_DK_T, GLA_DK_T, GLA_DV_T, GLA_DV_T, 2 * GLA_DECAY_RANK,
            NA_W, NA_W, NA_W,
            D_MODEL, D_MODEL, D_MODEL)
N_IN = sum(IN_SIZES)

kernel_name = 'hybrid_gated_branch_flow_block'


def split_cols(p):
    offs = [int(o) for o in np.cumsum(IN_SIZES)[:-1]]
    return jnp.split(p, offs, axis=-1)


def rmsnorm(x, g):
    xf = x.astype(jnp.float32)
    y = xf * lax.rsqrt(jnp.mean(xf * xf, axis=-1, keepdims=True) + EPS)
    return (y * g.astype(jnp.float32)).astype(x.dtype)


def sub_in(t, m, i, g):
    return rmsnorm(t, g) * (1 + m[..., 3 * i + 1, :, :]) + m[..., 3 * i, :, :]


def gate_of(m, i):
    return m[..., 3 * i + 2, :, :]


def swiglu(u, w_up, w_down):
    a, b = jnp.split(u @ w_up, 2, axis=-1)
    return (jax.nn.silu(a) * b) @ w_down


def short_conv(u, w):
    return lax.conv_general_dilated(u, w[:, None, :], window_strides=(1,),
                                    padding=[(CONV_K // 2, CONV_K // 2)],
                                    dimension_numbers=('NWC', 'WIO', 'NWC'),
                                    feature_group_count=u.shape[-1])


def conv_branch(h, bg, cg, w, wb):
    return (bg * short_conv(cg * h, w)) @ wb


def split_heads(t, nh):
    return t.reshape(t.shape[0], t.shape[1], nh, -1)


def _flip(t):
    return jnp.flip(t, axis=2)


def _rotate(u, pos):
    n = u.shape[-1] // 2
    freq = ROPE_THETA ** (-jnp.arange(n, dtype=jnp.float32) / n)
    ang = pos.astype(jnp.float32)[:, None] * freq[None, :]
    cos = jnp.cos(ang)[None, :, None, :]
    sin = jnp.sin(ang)[None, :, None, :]
    u1 = u[..., :n].astype(jnp.float32)
    u2 = u[..., n:].astype(jnp.float32)
    return jnp.concatenate([u1 * cos - u2 * sin, u1 * sin + u2 * cos], axis=-1)


def axial_rope(t):
    pos = jnp.arange(t.shape[1])
    half = t.shape[-1] // 2
    out = jnp.concatenate([_rotate(t[..., :half], pos // GRID_W),
                           _rotate(t[..., half:], pos % GRID_W)], axis=-1)
    return out.astype(t.dtype)


def gla_log_decay(lr, w, b):
    bsz, seq, _ = lr.shape
    la = jax.nn.log_sigmoid((lr @ w + b).astype(jnp.float32)) / GLA_TAU
    return la.reshape(bsz, seq, GLA_HEADS, GLA_DK).transpose(0, 2, 1, 3)


def gla_chunked(q, k, v, log_a, s0):
    bsz, nh, seq, dk = q.shape
    dv = v.shape[-1]
    n = seq // GLA_CHUNK
    q, k, log_a = (t.reshape(bsz, nh, n, GLA_CHUNK, dk) for t in (q, k, log_a))
    v = v.reshape(bsz, nh, n, GLA_CHUNK, dv)
    cum = jnp.cumsum(log_a, axis=3)
    last = cum[:, :, :, -1:, :]
    q_dec = q * jnp.exp(cum)
    att = jnp.einsum('bhnid,bhnjd->bhnij', q_dec, k * jnp.exp(-cum))
    tri = jnp.tril(jnp.ones((GLA_CHUNK, GLA_CHUNK), dtype=bool))
    att = jnp.where(tri, att, 0.0)
    o_intra = jnp.einsum('bhnij,bhnjv->bhniv', att, v)
    kv = jnp.einsum('bhnjd,bhnjv->bhndv', k * jnp.exp(last - cum), v)

    def step(s, inp):
        dec, kv_n = inp
        return s * dec[..., None] + kv_n, s

    _, s_start = lax.scan(step, s0, (jnp.moveaxis(jnp.exp(last[:, :, :, 0, :]), 2, 0),
                                     jnp.moveaxis(kv, 2, 0)))
    o_inter = jnp.einsum('bhnid,bhndv->bhniv', q_dec, jnp.moveaxis(s_start, 0, 2))
    return (o_intra + o_inter).reshape(bsz, nh, seq, dv)


def gla_final_state(k, v, log_a):
    cum = jnp.cumsum(log_a, axis=2)
    return jnp.einsum('bhld,bhlv->bhdv', k * jnp.exp(cum[:, :, -1:, :] - cum), v)


def gla_branch(q, k, v, g, la_f, la_b, s_f, s_b, norm_g, wb):
    o = gla_chunked(q, k, v, la_f, s_f) + _flip(gla_chunked(_flip(q), _flip(k), _flip(v), _flip(la_b), s_b))
    o = o.transpose(0, 2, 1, 3)
    o = o * lax.rsqrt(jnp.mean(o * o, axis=-1, keepdims=True) + EPS) * norm_g.reshape(GLA_HEADS, GLA_DV).astype(jnp.float32)
    o = o.reshape(o.shape[0], o.shape[1], GLA_DV_T).astype(g.dtype) * jax.nn.silu(g)
    return o @ wb


def na_latent(q, k, v, kc, vc, rpb):
    bsz, seq, nh, hd = q.shape
    rows = seq // GRID_W
    kr = min(NA_KR, rows)
    ncb = GRID_W // NA_KC
    nloc = kr * 2 * NA_KC
    qg = q.reshape(bsz, rows, ncb, NA_KC, nh, hd)
    kg = k.reshape(bsz, rows, GRID_W, nh, hd)
    vg = v.reshape(bsz, rows, GRID_W, nh, hd)
    qcol = jnp.arange(ncb)[:, None] * NA_KC + jnp.arange(NA_KC)[None, :]
    win_c = jnp.clip(qcol - NA_KC // 2, 0, GRID_W - NA_KC)
    blk_c = jnp.minimum(win_c[:, 0], GRID_W - 2 * NA_KC)
    key_c = blk_c[:, None] + jnp.arange(2 * NA_KC)[None, :]
    col_valid = (key_c[:, None, :] >= win_c[..., None]) & (key_c[:, None, :] < win_c[..., None] + NA_KC)
    col_off = jnp.clip(key_c[:, None, :] - qcol[..., None] + NA_KC - 1, 0, 2 * NA_KC - 2)
    win_r = jnp.clip(jnp.arange(rows) - NA_KR // 2, 0, rows - kr)

    def row_block(args):
        r, rs, qb = args
        kb = lax.dynamic_slice_in_dim(kg, rs, kr, axis=1)[:, :, key_c]
        vb = lax.dynamic_slice_in_dim(vg, rs, kr, axis=1)[:, :, key_c]
        bias = rpb[:, rs + jnp.arange(kr) - r + NA_KR - 1][:, :, col_off]
        bias = bias.transpose(0, 2, 3, 1, 4).astype(jnp.float32)
        s_loc = jnp.einsum('bcihd,brcjhd->bhcirj', qb, kb).astype(jnp.float32) + bias
        s_loc = jnp.where(col_valid[:, :, None, :], s_loc, -jnp.inf)
        s_ctx = jnp.einsum('bcihd,bmhd->bhcim', qb, kc).astype(jnp.float32)
        p = jax.nn.softmax(jnp.concatenate([s_loc.reshape(bsz, nh, ncb, NA_KC, nloc), s_ctx], axis=-1), axis=-1)
        p = p.astype(v.dtype)
        p_loc = p[..., :nloc].reshape(bsz, nh, ncb, NA_KC, kr, 2 * NA_KC)
        o = jnp.einsum('bhcirj,brcjhd->bcihd', p_loc, vb) + jnp.einsum('bhcim,bmhd->bcihd', p[..., nloc:], vc)
        return o.reshape(bsz, GRID_W, nh, hd)

    out = lax.map(row_block, (jnp.arange(rows), win_r, jnp.moveaxis(qg, 1, 0)))
    return jnp.moveaxis(out, 0, 1).reshape(bsz, seq, nh * hd)


def na_context(q, k, v):
    s = jnp.einsum('bmhd,bnhd->bhmn', q, k).astype(jnp.float32)
    p = jax.nn.softmax(s, axis=-1).astype(v.dtype)
    o = jnp.einsum('bhmn,bnhd->bmhd', p, v)
    return o.reshape(o.shape[0], o.shape[1], -1)


def token_mixers(xm, cm, w_in, conv_w, decay_w, decay_b, gla_norm_g, na_rpb,
                 wb_conv, wb_gla, wb_na, w_out, ctx_out):
    (x_h, x_bg, x_cg, x_q, x_k, x_v, x_g, x_lr, x_nq, x_nk, x_nv, x_ga, x_gb, x_gc) = split_cols(xm @ w_in)
    (c_h, c_bg, c_cg, c_q, c_k, c_v, c_g, c_lr, c_nq, c_nk, c_nv, c_ga, c_gb, c_gc) = split_cols(cm @ w_in)
    r = GLA_DECAY_RANK
    q_scale = GLA_DK ** -0.5
    na_scale = NA_HD ** -0.5

    def to_bhld(t, nh):
        return split_heads(t, nh).transpose(0, 2, 1, 3).astype(jnp.float32)

    kc = to_bhld(c_k, GLA_HEADS)
    vc = to_bhld(c_v, GLA_HEADS)
    la_cf = gla_log_decay(c_lr[..., :r], decay_w[0], decay_b[0])
    la_cb = gla_log_decay(c_lr[..., r:], decay_w[1], decay_b[1])
    s_f = gla_final_state(kc, vc, la_cf)
    s_b = gla_final_state(_flip(kc), _flip(vc), _flip(la_cb))
    qx = axial_rope(split_heads(x_q, GLA_HEADS)).transpose(0, 2, 1, 3).astype(jnp.float32) * q_scale
    kx = axial_rope(split_heads(x_k, GLA_HEADS)).transpose(0, 2, 1, 3).astype(jnp.float32)
    vx = to_bhld(x_v, GLA_HEADS)
    la_xf = gla_log_decay(x_lr[..., :r], decay_w[0], decay_b[0])
    la_xb = gla_log_decay(x_lr[..., r:], decay_w[1], decay_b[1])
    y_b = gla_branch(qx, kx, vx, x_g, la_xf, la_xb, s_f, s_b, gla_norm_g, wb_gla)
    nkc = split_heads(c_nk, NA_HEADS)
    nvc = split_heads(c_nv, NA_HEADS)
    y_c = na_latent(split_heads(x_nq, NA_HEADS) * na_scale, split_heads(x_nk, NA_HEADS),
                    split_heads(x_nv, NA_HEADS), nkc, nvc, na_rpb) @ wb_na
    y_a = conv_branch(x_h, x_bg, x_cg, conv_w, wb_conv)
    y_x = (jax.nn.sigmoid(x_ga) * y_a + jax.nn.sigmoid(x_gb) * y_b + jax.nn.sigmoid(x_gc) * y_c) @ w_out
    if not ctx_out:
        return y_x, None
    zeros = jnp.zeros_like(s_f)
    qc = to_bhld(c_q, GLA_HEADS) * q_scale
    yc_b = gla_branch(qc, kc, vc, c_g, la_cf, la_cb, zeros, zeros, gla_norm_g, wb_gla)
    yc_c = na_context(split_heads(c_nq, NA_HEADS) * na_scale, nkc, nvc) @ wb_na
    yc_a = conv_branch(c_h, c_bg, c_cg, conv_w, wb_conv)
    y_cx = (jax.nn.sigmoid(c_ga) * yc_a + jax.nn.sigmoid(c_gb) * yc_b + jax.nn.sigmoid(c_gc) * yc_c) @ w_out
    return y_x, y_cx


def setup_inputs(seed: int = 0) -> dict:
    key = jax.random.key(seed)
    ks = jax.random.split(key, 24)

    def nrm(k, shape, scale):
        return jax.random.normal(k, shape, jnp.float32) * scale

    return {
        'x': nrm(ks[0], (BATCH, SEQ, D_MODEL), 1.0),
        'c': nrm(ks[1], (BATCH, D_MODEL), 1.0),
        'ctx': nrm(ks[2], (BATCH, CTX_LEN, D_MODEL), 1.0),
        'c_ctx': nrm(ks[3], (D_MODEL,), 1.0),
        'mod_a': nrm(ks[4], (DEPTH, D_MODEL, MOD_RANK), D_MODEL ** -0.5),
        'mod_b': nrm(ks[5], (DEPTH, MOD_RANK, N_MOD * D_MODEL), 0.5 * MOD_RANK ** -0.5),
        'mod_bias': nrm(ks[6], (DEPTH, N_MOD * D_MODEL), 0.02),
        'norm_g': 1.0 + nrm(ks[7], (DEPTH, 3, D_MODEL), 0.02),
        'ffn_up': nrm(ks[8], (DEPTH, 2, D_MODEL, 2 * D_FF), D_MODEL ** -0.5),
        'ffn_down': nrm(ks[9], (DEPTH, 2, D_FF, D_MODEL), D_FF ** -0.5),
        'w_in': nrm(ks[10], (DEPTH, D_MODEL, N_IN), D_MODEL ** -0.5),
        'conv_w': nrm(ks[11], (DEPTH, CONV_K, CONV_W), CONV_K ** -0.5),
        'gla_decay_w': nrm(ks[12], (DEPTH, 2, GLA_DECAY_RANK, GLA_DK_T), GLA_DECAY_RANK ** -0.5),
        'gla_decay_b': nrm(ks[13], (DEPTH, 2, GLA_DK_T), 0.1),
        'gla_norm_g': 1.0 + nrm(ks[14], (DEPTH, GLA_DV_T), 0.02),
        'na_rpb': nrm(ks[15], (DEPTH, NA_HEADS, 2 * NA_KR - 1, 2 * NA_KC - 1), 0.1),
        'w_branch_conv': nrm(ks[16], (DEPTH, CONV_W, D_MODEL), CONV_W ** -0.5),
        'w_branch_gla': nrm(ks[17], (DEPTH, GLA_DV_T, D_MODEL), GLA_DV_T ** -0.5),
        'w_branch_na': nrm(ks[18], (DEPTH, NA_W, D_MODEL), NA_W ** -0.5),
        'w_out': nrm(ks[19], (DEPTH, D_MODEL, D_MODEL), D_MODEL ** -0.5),
        'final_g': 1.0 + nrm(ks[20], (D_MODEL,), 0.02),
    }


def reference(x, c, ctx, c_ctx, mod_a, mod_b, mod_bias, norm_g, ffn_up, ffn_down, w_in, conv_w,
              gla_decay_w, gla_decay_b, gla_norm_g, na_rpb, w_branch_conv, w_branch_gla, w_branch_na,
              w_out, final_g):
    h, hc = x, ctx
    sc = jax.nn.silu(c)
    scc = jax.nn.silu(c_ctx)
    for layer in range(DEPTH):
        last = layer == DEPTH - 1
        m_x = ((sc @ mod_a[layer]) @ mod_b[layer] + mod_bias[layer]).reshape(c.shape[0], N_MOD, 1, D_MODEL)
        m_c = ((scc @ mod_a[layer]) @ mod_b[layer] + mod_bias[layer]).reshape(N_MOD, 1, D_MODEL)
        up0, dn0 = ffn_up[layer, 0], ffn_down[layer, 0]
        h = h + 0.5 * gate_of(m_x, 0) * swiglu(sub_in(h, m_x, 0, norm_g[layer, 0]), up0, dn0)
        hc = hc + 0.5 * gate_of(m_c, 0) * swiglu(sub_in(hc, m_c, 0, norm_g[layer, 0]), up0, dn0)
        y_x, y_c = token_mixers(sub_in(h, m_x, 1, norm_g[layer, 1]), sub_in(hc, m_c, 1, norm_g[layer, 1]),
                                w_in[layer], conv_w[layer], gla_decay_w[layer], gla_decay_b[layer],
                                gla_norm_g[layer], na_rpb[layer], w_branch_conv[layer], w_branch_gla[layer],
                                w_branch_na[layer], w_out[layer], not last)
        h = h + gate_of(m_x, 1) * y_x
        up1, dn1 = ffn_up[layer, 1], ffn_down[layer, 1]
        h = h + 0.5 * gate_of(m_x, 2) * swiglu(sub_in(h, m_x, 2, norm_g[layer, 2]), up1, dn1)
        if not last:
            hc = hc + gate_of(m_c, 1) * y_c
            hc = hc + 0.5 * gate_of(m_c, 2) * swiglu(sub_in(hc, m_c, 2, norm_g[layer, 2]), up1, dn1)
    return rmsnorm(h, final_g)
```

```python
import functools

import numpy as np
import jax
import jax.numpy as jnp
from jax import lax
from jax.experimental import pallas as pl
from jax.experimental.pallas import tpu as pltpu

F32 = jnp.float32
BF16 = jnp.bfloat16

GRID_W = 64
N_MOD = 9
CONV_W = 1024
GLA_HEADS = 4
GLA_DK = 192
GLA_DV = 384
GLA_DKP = 256
GLA_RANK = 16
GLA_TAU = 16.0
GLA_CHUNK = 64
NA_HEADS = 12
NA_HD = 128
NA_W = NA_HEADS * NA_HD
NA_KR = 8
NA_KC = 16
NA_GROUP = 4
NA_WIN = NA_KR + NA_GROUP - 1
ROPE_THETA = 10000.0
EPS = 1e-6

P_H, P_BG, P_CG = 0, 1024, 2048
P_NQ, P_NK = 3072, 4608
P_V, P_G = 6144, 7680
P_NV = 9216
P_Q, P_K = 10752, 11776
P_GA = 12800
LR_PAD = 512

VMEM_LIMIT = 56 * 1024 * 1024
TM = 1152
TN = 512


def _cparams(n_axes):
    return pltpu.CompilerParams(dimension_semantics=("arbitrary",) * n_axes,
                                vmem_limit_bytes=VMEM_LIMIT)


def _dot(a, b):
    return jnp.dot(a, b, preferred_element_type=F32)


def _dot_nt(a, b):
    return lax.dot_general(a, b, (((1,), (1,)), ((), ())), preferred_element_type=F32)


def _dot_tn(a, b):
    return lax.dot_general(a, b, (((0,), (0,)), ((), ())), preferred_element_type=F32)


def _mod_a_kernel(c_ref, a_ref, o_ref):
    c = c_ref[...]
    s = (c * jax.nn.sigmoid(c)).astype(BF16)
    o_ref[...] = _dot(s, a_ref[...].astype(BF16))


def _mod_b_kernel(t_ref, b_ref, bias_ref, o_ref):
    o_ref[...] = _dot(t_ref[...].astype(BF16), b_ref[...].astype(BF16)) + bias_ref[...]


def _modulation(cvec, mod_a, mod_b, mod_bias):
    depth, d, rank = mod_a.shape
    rows = cvec.shape[0]
    nm = mod_b.shape[-1]
    tn = min(d, 2048)
    assert nm % tn == 0
    t = pl.pallas_call(
        _mod_a_kernel,
        grid=(depth,),
        in_specs=[pl.BlockSpec((rows, d), lambda l: (0, 0)),
                  pl.BlockSpec((None, d, rank), lambda l: (l, 0, 0))],
        out_specs=pl.BlockSpec((None, rows, rank), lambda l: (l, 0, 0)),
        out_shape=jax.ShapeDtypeStruct((depth, rows, rank), F32),
        compiler_params=_cparams(1),
        name="mod_a",
    )(cvec, mod_a)
    return pl.pallas_call(
        _mod_b_kernel,
        grid=(depth, nm // tn),
        in_specs=[pl.BlockSpec((None, rows, rank), lambda l, j: (l, 0, 0)),
                  pl.BlockSpec((None, rank, tn), lambda l, j: (l, 0, j)),
                  pl.BlockSpec((None, 1, tn), lambda l, j: (l, 0, j))],
        out_specs=pl.BlockSpec((None, rows, tn), lambda l, j: (l, 0, j)),
        out_shape=jax.ShapeDtypeStruct((depth, rows, nm), F32),
        compiler_params=_cparams(2),
        name="mod_b",
    )(t, mod_b, mod_bias.reshape(depth, 1, nm))


def _norm_mod_kernel(h_ref, g_ref, sh_ref, sc_ref, o_ref):
    x = h_ref[...]
    y = x * lax.rsqrt(jnp.mean(x * x, axis=-1, keepdims=True) + EPS) * g_ref[...]
    o_ref[...] = (y * (1.0 + sc_ref[...]) + sh_ref[...]).astype(o_ref.dtype)


def _norm_mod(h, g, m, sub, ctx_len, slab):
    mrows, d = h.shape
    tr = ctx_len
    per = slab // tr

    def mrow(j):
        return jnp.where(j % per == 0, 0, 1 + j // per)

    return pl.pallas_call(
        _norm_mod_kernel,
        grid=(mrows // tr,),
        in_specs=[pl.BlockSpec((tr, d), lambda j: (j, 0)),
                  pl.BlockSpec((1, d), lambda j: (0, 0)),
                  pl.BlockSpec((None, 1, d), lambda j: (mrow(j), 0, 3 * sub)),
                  pl.BlockSpec((None, 1, d), lambda j: (mrow(j), 0, 3 * sub + 1))],
        out_specs=pl.BlockSpec((tr, d), lambda j: (j, 0)),
        out_shape=jax.ShapeDtypeStruct((mrows, d), BF16),
        compiler_params=_cparams(1),
        name="norm_mod",
    )(h, g, m, m)


def _final_norm_kernel(h_ref, g_ref, o_ref):
    x = h_ref[...]
    o_ref[...] = x * lax.rsqrt(jnp.mean(x * x, axis=-1, keepdims=True) + EPS) * g_ref[...]


def _final_norm(h, g, bsz, ctx_len, seq):
    d = h.shape[-1]
    slab = ctx_len + seq
    tr = ctx_len
    return pl.pallas_call(
        _final_norm_kernel,
        grid=(bsz, seq // tr),
        in_specs=[pl.BlockSpec((None, tr, d), lambda b, j: (b, 1 + j, 0)),
                  pl.BlockSpec((1, d), lambda b, j: (0, 0))],
        out_specs=pl.BlockSpec((None, tr, d), lambda b, j: (b, j, 0)),
        out_shape=jax.ShapeDtypeStruct((bsz, seq, d), F32),
        compiler_params=_cparams(2),
        name="final_norm",
    )(h.reshape(bsz, slab, d), g)


def _proj_kernel(a_ref, w_ref, o_ref):
    o_ref[...] = _dot(a_ref[...], w_ref[...]).astype(o_ref.dtype)


def _proj(a, w):
    mrows, k = a.shape
    n = w.shape[1]
    return pl.pallas_call(
        _proj_kernel,
        grid=(mrows // TM, n // TN),
        in_specs=[pl.BlockSpec((TM, k), lambda i, j: (i, 0)),
                  pl.BlockSpec((k, TN), lambda i, j: (0, j))],
        out_specs=pl.BlockSpec((TM, TN), lambda i, j: (i, j)),
        out_shape=jax.ShapeDtypeStruct((mrows, n), BF16),
        compiler_params=_cparams(2),
        name="in_proj",
    )(a, w)


def _swiglu_kernel(u_ref, wa_ref, wb_ref, o_ref):
    u = u_ref[...]
    a = _dot(u, wa_ref[...])
    b = _dot(u, wb_ref[...])
    o_ref[...] = (a * jax.nn.sigmoid(a) * b).astype(o_ref.dtype)


def _ffn_up(u, w_up):
    mrows, k = u.shape
    dff = w_up.shape[1] // 2
    nb = dff // TN
    return pl.pallas_call(
        _swiglu_kernel,
        grid=(mrows // TM, nb),
        in_specs=[pl.BlockSpec((TM, k), lambda i, j: (i, 0)),
                  pl.BlockSpec((k, TN), lambda i, j: (0, j)),
                  pl.BlockSpec((k, TN), lambda i, j: (0, nb + j))],
        out_specs=pl.BlockSpec((TM, TN), lambda i, j: (i, j)),
        out_shape=jax.ShapeDtypeStruct((mrows, dff), BF16),
        compiler_params=_cparams(2),
        name="ffn_up",
    )(u, w_up, w_up)


def _resid_kernel(a_ref, w_ref, h_ref, gc_ref, gx_ref, o_ref, *, scale, ctx_len, slab):
    tm = a_ref.shape[0]
    acc = _dot(a_ref[...], w_ref[...])
    row0 = (pl.program_id(0) * tm) % slab
    rows = lax.broadcasted_iota(jnp.int32, (tm, 1), 0) + row0
    gate = jnp.where(rows < ctx_len, gc_ref[...], gx_ref[...])
    o_ref[...] = h_ref[...] + (scale * gate) * acc


def _resid_proj(a, w, h, m, sub, scale, ctx_len, slab):
    mrows, k = a.shape
    n = w.shape[1]
    g0 = (3 * sub + 2) * (n // TN)
    return pl.pallas_call(
        functools.partial(_resid_kernel, scale=scale, ctx_len=ctx_len, slab=slab),
        grid=(mrows // TM, n // TN),
        in_specs=[pl.BlockSpec((TM, k), lambda i, j: (i, 0)),
                  pl.BlockSpec((k, TN), lambda i, j: (0, j)),
                  pl.BlockSpec((TM, TN), lambda i, j: (i, j)),
                  pl.BlockSpec((None, 1, TN), lambda i, j: (0, 0, g0 + j)),
                  pl.BlockSpec((None, 1, TN), lambda i, j: (1 + (i * TM) // slab, 0, g0 + j))],
        out_specs=pl.BlockSpec((TM, TN), lambda i, j: (i, j)),
        out_shape=jax.ShapeDtypeStruct((mrows, n), F32),
        input_output_aliases={2: 0},
        compiler_params=_cparams(2),
        name="resid_proj",
    )(a, w, h, m, m)


def _branch_kernel(a_ref, b_ref, c_ref, pa_ref, pb_ref, pc_ref, ga_ref, gb_ref, gc_ref, o_ref):
    ya = _dot(a_ref[...], pa_ref[...])
    yb = _dot(b_ref[...], pb_ref[...])
    yc = _dot(c_ref[...], pc_ref[...])
    z = (jax.nn.sigmoid(ga_ref[...].astype(F32)) * ya
         + jax.nn.sigmoid(gb_ref[...].astype(F32)) * yb
         + jax.nn.sigmoid(gc_ref[...].astype(F32)) * yc)
    o_ref[...] = z.astype(o_ref.dtype)


def _branch_proj(a, b, c, pa, pb, pc, p):
    mrows = a.shape[0]
    d = pa.shape[1]
    tn = 256
    g0 = P_GA // tn
    nb = d // tn
    return pl.pallas_call(
        _branch_kernel,
        grid=(mrows // TM, nb),
        in_specs=[pl.BlockSpec((TM, a.shape[1]), lambda i, j: (i, 0)),
                  pl.BlockSpec((TM, b.shape[1]), lambda i, j: (i, 0)),
                  pl.BlockSpec((TM, c.shape[1]), lambda i, j: (i, 0)),
                  pl.BlockSpec((pa.shape[0], tn), lambda i, j: (0, j)),
                  pl.BlockSpec((pb.shape[0], tn), lambda i, j: (0, j)),
                  pl.BlockSpec((pc.shape[0], tn), lambda i, j: (0, j)),
                  pl.BlockSpec((TM, tn), lambda i, j: (i, g0 + j)),
                  pl.BlockSpec((TM, tn), lambda i, j: (i, g0 + nb + j)),
                  pl.BlockSpec((TM, tn), lambda i, j: (i, g0 + 2 * nb + j))],
        out_specs=pl.BlockSpec((TM, tn), lambda i, j: (i, j)),
        out_shape=jax.ShapeDtypeStruct((mrows, d), BF16),
        compiler_params=_cparams(2),
        name="branch_proj",
    )(a, b, c, pa, pb, pc, p, p, p)


def _conv_kernel(h_ref, bg_ref, cg_ref, w_ref, o_ref, *, ctx_len):
    n = h_ref.shape[0]
    x = cg_ref[...].astype(F32) * h_ref[...].astype(F32)
    row = lax.broadcasted_iota(jnp.int32, (n, 1), 0)
    first = (row == 0) | (row == ctx_len)
    last = (row == ctx_len - 1) | (row == n - 1)
    prev = jnp.where(first, 0.0, pltpu.roll(x, 1, axis=0))
    nxt = jnp.where(last, 0.0, pltpu.roll(x, n - 1, axis=0))
    w = w_ref[...]
    y = prev * w[0:1, :] + x * w[1:2, :] + nxt * w[2:3, :]
    o_ref[...] = (bg_ref[...].astype(F32) * y).astype(o_ref.dtype)


def _conv_branch(p, conv_w, bsz, ctx_len, slab):
    tc = 256
    nb = CONV_W // tc
    return pl.pallas_call(
        functools.partial(_conv_kernel, ctx_len=ctx_len),
        grid=(bsz, nb),
        in_specs=[pl.BlockSpec((slab, tc), lambda b, j: (b, P_H // tc + j)),
                  pl.BlockSpec((slab, tc), lambda b, j: (b, P_BG // tc + j)),
                  pl.BlockSpec((slab, tc), lambda b, j: (b, P_CG // tc + j)),
                  pl.BlockSpec((conv_w.shape[0], tc), lambda b, j: (0, j))],
        out_specs=pl.BlockSpec((slab, tc), lambda b, j: (b, j)),
        out_shape=jax.ShapeDtypeStruct((bsz * slab, CONV_W), BF16),
        compiler_params=_cparams(2),
        name="conv_branch",
    )(p, p, p, conv_w)


def _log_sigmoid(x):
    return jnp.minimum(x, 0.0) - jnp.log1p(jnp.exp(-jnp.abs(x)))


def _gla_kernel(q_ref, k_ref, v_ref, g_ref, lr_ref, cos_ref, sin_ref, wdf_ref, wdb_ref,
                bdf_ref, bdb_ref, ng_ref, o_ref, qd_s, ki_s, ke_s, dec_s, oacc_s, st_s,
                *, ctx_len):
    n = q_ref.shape[0]
    nchunk = n // GLA_CHUNK
    cchunk = ctx_len // GLA_CHUNK
    half = GLA_DKP // 2
    cos = cos_ref[...]
    sin = sin_ref[...]

    def rope(x):
        x1 = x[:, :half]
        x2 = x[:, half:]
        return jnp.concatenate([x1 * cos - x2 * sin, x1 * sin + x2 * cos], axis=1)

    qr = rope(q_ref[...].astype(F32)) * (GLA_DK ** -0.5)
    kr = rope(k_ref[...].astype(F32))
    lr = lr_ref[...]
    ridx = lax.broadcasted_iota(jnp.int32, (n, 1), 0) % GLA_CHUNK
    ci = lax.broadcasted_iota(jnp.int32, (GLA_CHUNK, GLA_CHUNK), 0)
    cj = lax.broadcasted_iota(jnp.int32, (GLA_CHUNK, GLA_CHUNK), 1)

    def chunk_sums(la):
        cum = la
        for sft in (1, 2, 4, 8, 16, 32):
            cum = cum + jnp.where(ridx >= sft, pltpu.roll(cum, sft, axis=0), 0.0)
        c3 = cum.reshape(nchunk, GLA_CHUNK, GLA_DKP)
        tot3 = c3[:, GLA_CHUNK - 1:GLA_CHUNK, :]
        tot = jnp.broadcast_to(tot3, c3.shape).reshape(n, GLA_DKP)
        return cum, tot, tot3

    def stage(cum, tot, tot3):
        qd_s[...] = (qr * jnp.exp(cum)).astype(BF16)
        ki_s[...] = (kr * jnp.exp(-cum)).astype(BF16)
        ke_s[...] = (kr * jnp.exp(tot - cum)).astype(BF16)
        dec_s[...] = jnp.exp(tot3)

    def scan(chunk_of, steps, mask, first):
        def body(i, carry):
            c = chunk_of(i)
            r0 = pl.multiple_of(c * GLA_CHUNK, GLA_CHUNK)
            qd = qd_s[pl.ds(r0, GLA_CHUNK), :]
            ki = ki_s[pl.ds(r0, GLA_CHUNK), :]
            ke = ke_s[pl.ds(r0, GLA_CHUNK), :]
            vv = v_ref[pl.ds(r0, GLA_CHUNK), :]
            att = jnp.where(mask, _dot_nt(qd, ki), 0.0)
            st = st_s[...]
            o = _dot(att.astype(BF16), vv) + _dot_nt(qd, st.astype(BF16))
            if first:
                oacc_s[pl.ds(r0, GLA_CHUNK), :] = o
            else:
                oacc_s[pl.ds(r0, GLA_CHUNK), :] += o
            st_s[...] = st * dec_s[c] + _dot_tn(vv, ke)
            return carry
        lax.fori_loop(0, steps, body, 0)

    la = _log_sigmoid(_dot(lr, wdf_ref[...]) + bdf_ref[...]) / GLA_TAU
    cum, tot, tot3 = chunk_sums(la)
    stage(cum, tot, tot3)
    st_s[...] = jnp.zeros_like(st_s)
    scan(lambda i: i, nchunk, ci >= cj, True)

    la = _log_sigmoid(_dot(lr, wdb_ref[...]) + bdb_ref[...]) / GLA_TAU
    cum, tot, tot3 = chunk_sums(la)
    stage(tot - cum + la, tot, tot3)
    st_s[...] = jnp.zeros_like(st_s)
    scan(lambda i: cchunk - 1 - i, cchunk, ci <= cj, False)
    scan(lambda i: nchunk - 1 - i, nchunk - cchunk, ci <= cj, False)

    o = oacc_s[...]
    o = o * lax.rsqrt(jnp.mean(o * o, axis=-1, keepdims=True) + EPS) * ng_ref[...]
    g = g_ref[...].astype(F32)
    o_ref[...] = (o * (g * jax.nn.sigmoid(g))).astype(o_ref.dtype)


def _gla_branch(p, cos, sin, wdf, wdb, bdf, bdb, ng, bsz, ctx_len, slab):
    lr_col = p.shape[1] - LR_PAD
    nchunk = slab // GLA_CHUNK
    head = lambda b, h: (h, 0, 0)
    return pl.pallas_call(
        functools.partial(_gla_kernel, ctx_len=ctx_len),
        grid=(bsz, GLA_HEADS),
        in_specs=[pl.BlockSpec((slab, GLA_DKP), lambda b, h: (b, P_Q // GLA_DKP + h)),
                  pl.BlockSpec((slab, GLA_DKP), lambda b, h: (b, P_K // GLA_DKP + h)),
                  pl.BlockSpec((slab, GLA_DV), lambda b, h: (b, P_V // GLA_DV + h)),
                  pl.BlockSpec((slab, GLA_DV), lambda b, h: (b, P_G // GLA_DV + h)),
                  pl.BlockSpec((slab, 128), lambda b, h: (b, lr_col // 128)),
                  pl.BlockSpec((slab, GLA_DKP // 2), lambda b, h: (0, 0)),
                  pl.BlockSpec((slab, GLA_DKP // 2), lambda b, h: (0, 0)),
                  pl.BlockSpec((None, 128, GLA_DKP), head),
                  pl.BlockSpec((None, 128, GLA_DKP), head),
                  pl.BlockSpec((None, 1, GLA_DKP), head),
                  pl.BlockSpec((None, 1, GLA_DKP), head),
                  pl.BlockSpec((None, 1, GLA_DV), head)],
        out_specs=pl.BlockSpec((slab, GLA_DV), lambda b, h: (b, h)),
        out_shape=jax.ShapeDtypeStruct((bsz * slab, GLA_HEADS * GLA_DV), BF16),
        scratch_shapes=[pltpu.VMEM((slab, GLA_DKP), BF16),
                        pltpu.VMEM((slab, GLA_DKP), BF16),
                        pltpu.VMEM((slab, GLA_DKP), BF16),
                        pltpu.VMEM((nchunk, 1, GLA_DKP), F32),
                        pltpu.VMEM((slab, GLA_DV), F32),
                        pltpu.VMEM((GLA_DV, GLA_DKP), F32)],
        compiler_params=_cparams(2),
        name="gla_branch",
    )(p, p, p, p, p, cos, sin, wdf, wdb, bdf, bdb, ng)


def _na_kernel(q_ref, k_ref, v_ref, bias_ref, o_ref, *, ctx_len, rows):
    scale = NA_HD ** -0.5
    gq = NA_GROUP * GRID_W
    nwin = NA_WIN * GRID_W
    kc = k_ref[0:ctx_len, :]
    vc = v_ref[0:ctx_len, :]

    s = _dot_nt(q_ref[0:ctx_len, :], kc) * scale
    e = jnp.exp(s - jnp.max(s, axis=-1, keepdims=True))
    pr = e / jnp.sum(e, axis=-1, keepdims=True)
    o_ref[0:ctx_len, :] = _dot(pr.astype(BF16), vc).astype(o_ref.dtype)

    ngroups = rows // NA_GROUP
    for g in range(ngroups):
        kind = 0 if g == 0 else (2 if g == ngroups - 1 else 1)
        q0 = ctx_len + g * gq
        k0 = ctx_len + min(max(g * NA_GROUP - NA_KR // 2, 0), rows - NA_WIN) * GRID_W
        qg = q_ref[q0:q0 + gq, :]
        s_loc = _dot_nt(qg, k_ref[k0:k0 + nwin, :]) * scale + bias_ref[kind]
        s_ctx = _dot_nt(qg, kc) * scale
        m = jnp.maximum(jnp.max(s_loc, axis=-1, keepdims=True), jnp.max(s_ctx, axis=-1, keepdims=True))
        e_loc = jnp.exp(s_loc - m)
        e_ctx = jnp.exp(s_ctx - m)
        inv = 1.0 / (jnp.sum(e_loc, axis=-1, keepdims=True) + jnp.sum(e_ctx, axis=-1, keepdims=True))
        o = _dot((e_loc * inv).astype(BF16), v_ref[k0:k0 + nwin, :]) + _dot((e_ctx * inv).astype(BF16), vc)
        o_ref[q0:q0 + gq, :] = o.astype(o_ref.dtype)


def _na_branch(p, bias, bsz, ctx_len, slab):
    rows = (slab - ctx_len) // GRID_W
    return pl.pallas_call(
        functools.partial(_na_kernel, ctx_len=ctx_len, rows=rows),
        grid=(NA_HEADS, bsz),
        in_specs=[pl.BlockSpec((slab, NA_HD), lambda h, b: (b, P_NQ // NA_HD + h)),
                  pl.BlockSpec((slab, NA_HD), lambda h, b: (b, P_NK // NA_HD + h)),
                  pl.BlockSpec((slab, NA_HD), lambda h, b: (b, P_NV // NA_HD + h)),
                  pl.BlockSpec((None,) + bias.shape[1:], lambda h, b: (h, 0, 0, 0))],
        out_specs=pl.BlockSpec((slab, NA_HD), lambda h, b: (b, h)),
        out_shape=jax.ShapeDtypeStruct((bsz * slab, NA_W), BF16),
        compiler_params=_cparams(2),
        name="na_branch",
    )(p, p, p, bias)


def _gla_dk_order():
    q = GLA_DK // 4
    one = np.full((GLA_DKP,), -1, np.int64)
    one[0:q] = np.arange(0, q)
    one[q:2 * q] = np.arange(2 * q, 3 * q)
    one[GLA_DKP // 2:GLA_DKP // 2 + q] = np.arange(q, 2 * q)
    one[GLA_DKP // 2 + q:GLA_DKP // 2 + 2 * q] = np.arange(3 * q, 4 * q)
    out = np.concatenate([np.where(one >= 0, one + h * GLA_DK, -1) for h in range(GLA_HEADS)])
    return out


def _take_cols(w, cols):
    pieces = []
    i = 0
    n = len(cols)
    while i < n:
        j = i + 1
        if cols[i] < 0:
            while j < n and cols[j] < 0:
                j += 1
            pieces.append(jnp.zeros(w.shape[:-1] + (j - i,), w.dtype))
        else:
            while j < n and cols[j] == cols[j - 1] + 1:
                j += 1
            pieces.append(w[..., int(cols[i]):int(cols[i]) + (j - i)])
        i = j
    return jnp.concatenate(pieces, axis=-1)


def _in_proj_cols(d):
    o_h, o_bg, o_cg = 0, CONV_W, 2 * CONV_W
    o_q = 3 * CONV_W
    o_k = o_q + GLA_HEADS * GLA_DK
    o_v = o_k + GLA_HEADS * GLA_DK
    o_g = o_v + GLA_HEADS * GLA_DV
    o_lr = o_g + GLA_HEADS * GLA_DV
    o_nq = o_lr + 2 * GLA_RANK
    o_nk = o_nq + NA_W
    o_nv = o_nk + NA_W
    o_ga = o_nv + NA_W
    total = P_GA + 3 * d + LR_PAD
    cols = np.full((total,), -1, np.int64)

    def put(dst, src, width):
        cols[dst:dst + width] = np.arange(src, src + width)

    put(P_H, o_h, CONV_W)
    put(P_BG, o_bg, CONV_W)
    put(P_CG, o_cg, CONV_W)
    put(P_NQ, o_nq, NA_W)
    put(P_NK, o_nk, NA_W)
    put(P_V, o_v, GLA_HEADS * GLA_DV)
    put(P_G, o_g, GLA_HEADS * GLA_DV)
    put(P_NV, o_nv, NA_W)
    order = _gla_dk_order()
    cols[P_Q:P_Q + len(order)] = np.where(order >= 0, order + o_q, -1)
    cols[P_K:P_K + len(order)] = np.where(order >= 0, order + o_k, -1)
    put(P_GA, o_ga, 3 * d)
    put(P_GA + 3 * d, o_lr, 2 * GLA_RANK)
    return cols


def _rope_tables(ctx_len, seq):
    q = GLA_DK // 4
    pos = jnp.arange(seq)
    freq = ROPE_THETA ** (-jnp.arange(q, dtype=F32) / q)
    ang_r = (pos // GRID_W).astype(F32)[:, None] * freq[None, :]
    ang_c = (pos % GRID_W).astype(F32)[:, None] * freq[None, :]
    pad = GLA_DKP // 2 - 2 * q
    cos = jnp.concatenate([jnp.cos(ang_r), jnp.cos(ang_c), jnp.ones((seq, pad), F32)], axis=1)
    sin = jnp.concatenate([jnp.sin(ang_r), jnp.sin(ang_c), jnp.zeros((seq, pad), F32)], axis=1)
    cos = jnp.concatenate([jnp.ones((ctx_len, GLA_DKP // 2), F32), cos], axis=0)
    sin = jnp.concatenate([jnp.zeros((ctx_len, GLA_DKP // 2), F32), sin], axis=0)
    return cos, sin


def _na_bias_tables(rpb, rows):
    ngroups = rows // NA_GROUP
    tabs = []
    qi = np.arange(NA_GROUP * GRID_W)
    kj = np.arange(NA_WIN * GRID_W)
    q_r, q_c = qi // GRID_W, qi % GRID_W
    k_r, k_c = kj // GRID_W, kj % GRID_W
    win_c = np.clip(q_c - NA_KC // 2, 0, GRID_W - NA_KC)
    col_ok = (k_c[None, :] >= win_c[:, None]) & (k_c[None, :] < win_c[:, None] + NA_KC)
    col_off = np.clip(k_c[None, :] - q_c[:, None] + NA_KC - 1, 0, 2 * NA_KC - 2)
    for g in (0, 1, ngroups - 1):
        base = min(max(g * NA_GROUP - NA_KR // 2, 0), rows - NA_WIN)
        r_abs = g * NA_GROUP + q_r
        win_r = np.clip(r_abs - NA_KR // 2, 0, rows - NA_KR)
        key_r = base + k_r
        row_ok = (key_r[None, :] >= win_r[:, None]) & (key_r[None, :] < win_r[:, None] + NA_KR)
        row_off = np.clip(key_r[None, :] - r_abs[:, None] + NA_KR - 1, 0, 2 * NA_KR - 2)
        vals = rpb[:, :, row_off, col_off].astype(F32)
        tabs.append(jnp.where(jnp.asarray(row_ok & col_ok), vals, -jnp.inf))
    return jnp.stack(tabs, axis=2)


def kernel(x, c, ctx, c_ctx, mod_a, mod_b, mod_bias, norm_g, ffn_up, ffn_down, w_in, conv_w,
           gla_decay_w, gla_decay_b, gla_norm_g, na_rpb, w_branch_conv, w_branch_gla, w_branch_na,
           w_out, final_g):
    bsz, seq, d = x.shape
    ctx_len = ctx.shape[1]
    depth = mod_a.shape[0]
    slab = ctx_len + seq
    rows = seq // GRID_W
    assert slab % TM == 0 and d % TN == 0 and seq % ctx_len == 0
    assert rows % NA_GROUP == 0 and rows >= NA_WIN and ctx_len % GLA_CHUNK == 0

    mrows = 16
    cvec = jnp.concatenate([c_ctx[None, :], c, jnp.zeros((mrows - 1 - bsz, d), F32)], axis=0)
    mods = _modulation(cvec, mod_a, mod_b, mod_bias)[:, :1 + bsz].reshape(depth, 1 + bsz, 1, N_MOD * d)

    w_in_p = _take_cols(w_in, _in_proj_cols(d)).astype(BF16)
    ffn_up_b = ffn_up.astype(BF16)
    ffn_down_b = ffn_down.astype(BF16)
    wb_conv = w_branch_conv.astype(BF16)
    wb_gla = w_branch_gla.astype(BF16)
    wb_na = w_branch_na.astype(BF16)
    w_out_b = w_out.astype(BF16)
    order = _gla_dk_order()
    dw = _take_cols(gla_decay_w, order).reshape(depth, 2, GLA_RANK, GLA_HEADS, GLA_DKP)
    dw = dw.transpose(0, 1, 3, 2, 4)
    wdf = jnp.zeros((depth, GLA_HEADS, 128, GLA_DKP), F32).at[:, :, 0:GLA_RANK].set(dw[:, 0])
    wdb = jnp.zeros((depth, GLA_HEADS, 128, GLA_DKP), F32).at[:, :, GLA_RANK:2 * GLA_RANK].set(dw[:, 1])
    wdf = wdf.astype(BF16)
    wdb = wdb.astype(BF16)
    db = _take_cols(gla_decay_b, order).reshape(depth, 2, GLA_HEADS, 1, GLA_DKP)
    ng = gla_norm_g.reshape(depth, GLA_HEADS, 1, GLA_DV)
    cos, sin = _rope_tables(ctx_len, seq)
    na_bias = _na_bias_tables(na_rpb, rows)

    h = jnp.concatenate([ctx, x], axis=1).reshape(bsz * slab, d)
    for l in range(depth):
        m = mods[l]
        nm = functools.partial(_norm_mod, m=m, ctx_len=ctx_len, slab=slab)
        rp = functools.partial(_resid_proj, m=m, ctx_len=ctx_len, slab=slab)
        act = _ffn_up(nm(h, norm_g[l, 0:1], sub=0), ffn_up_b[l, 0])
        h = rp(act, ffn_down_b[l, 0], h, sub=0, scale=0.5)
        p = _proj(nm(h, norm_g[l, 1:2], sub=1), w_in_p[l])
        ya = _conv_branch(p, conv_w[l], bsz, ctx_len, slab)
        yb = _gla_branch(p, cos, sin, wdf[l], wdb[l], db[l, 0], db[l, 1], ng[l], bsz, ctx_len, slab)
        yc = _na_branch(p, na_bias[l], bsz, ctx_len, slab)
        z = _branch_proj(ya, yb, yc, wb_conv[l], wb_gla[l], wb_na[l], p)
        h = rp(z, w_out_b[l], h, sub=1, scale=1.0)
        act = _ffn_up(nm(h, norm_g[l, 2:3], sub=2), ffn_up_b[l, 1])
        h = rp(act, ffn_down_b[l, 1], h, sub=2, scale=0.5)
    return _final_norm(h, final_g.reshape(1, d), bsz, ctx_len, seq)
```

```python
import functools

import numpy as np
import jax
import jax.numpy as jnp
from jax import lax
from jax.experimental import pallas as pl
from jax.experimental.pallas import tpu as pltpu

F32 = jnp.float32
BF16 = jnp.bfloat16

GRID_W = 64
N_MOD = 9
CONV_W = 1024
GLA_HEADS = 4
GLA_DK = 192
GLA_DV = 384
GLA_DKP = 256
GLA_RANK = 16
GLA_TAU = 16.0
GLA_CHUNK = 64
NA_HEADS = 12
NA_HD = 128
NA_W = NA_HEADS * NA_HD
NA_KR = 8
NA_KC = 16
NA_GROUP = 4
NA_WIN = NA_KR + NA_GROUP - 1
ROPE_THETA = 10000.0
EPS = 1e-6

P_H, P_BG, P_CG = 0, 1024, 2048
P_V, P_G = 3072, 4608
P_Q, P_K = 6144, 7168
P_LR = 8192
LR_PAD = 512
P1_W = P_LR + LR_PAD
P2_NQ, P2_NK, P2_NV = 0, 1536, 3072
P2_GA = 4608

VMEM_LIMIT = 56 * 1024 * 1024
TM = 1152
TN = 512


def _cparams(n_axes):
    return pltpu.CompilerParams(dimension_semantics=("arbitrary",) * n_axes,
                                vmem_limit_bytes=VMEM_LIMIT)


def _dot(a, b):
    return jnp.dot(a, b, preferred_element_type=F32)


def _dot_nt(a, b):
    return lax.dot_general(a, b, (((1,), (1,)), ((), ())), preferred_element_type=F32)


def _dot_tn(a, b):
    return lax.dot_general(a, b, (((0,), (0,)), ((), ())), preferred_element_type=F32)


def _mod_a_kernel(c_ref, a_ref, o_ref):
    c = c_ref[...]
    s = (c * jax.nn.sigmoid(c)).astype(BF16)
    o_ref[...] = _dot(s, a_ref[...].astype(BF16))


def _mod_b_kernel(t_ref, b_ref, bias_ref, o_ref):
    o_ref[...] = _dot(t_ref[...].astype(BF16), b_ref[...].astype(BF16)) + bias_ref[...]


def _modulation(cvec, mod_a, mod_b, mod_bias):
    depth, d, rank = mod_a.shape
    rows = cvec.shape[0]
    nm = mod_b.shape[-1]
    tn = min(d, 2048)
    assert nm % tn == 0
    t = pl.pallas_call(
        _mod_a_kernel,
        grid=(depth,),
        in_specs=[pl.BlockSpec((rows, d), lambda l: (0, 0)),
                  pl.BlockSpec((None, d, rank), lambda l: (l, 0, 0))],
        out_specs=pl.BlockSpec((None, rows, rank), lambda l: (l, 0, 0)),
        out_shape=jax.ShapeDtypeStruct((depth, rows, rank), F32),
        compiler_params=_cparams(1),
        name="mod_a",
    )(cvec, mod_a)
    return pl.pallas_call(
        _mod_b_kernel,
        grid=(depth, nm // tn),
        in_specs=[pl.BlockSpec((None, rows, rank), lambda l, j: (l, 0, 0)),
                  pl.BlockSpec((None, rank, tn), lambda l, j: (l, 0, j)),
                  pl.BlockSpec((None, 1, tn), lambda l, j: (l, 0, j))],
        out_specs=pl.BlockSpec((None, rows, tn), lambda l, j: (l, 0, j)),
        out_shape=jax.ShapeDtypeStruct((depth, rows, nm), F32),
        compiler_params=_cparams(2),
        name="mod_b",
    )(t, mod_b, mod_bias.reshape(depth, 1, nm))


def _norm_mod_kernel(h_ref, g_ref, sh_ref, sc_ref, o_ref):
    x = h_ref[...]
    y = x * lax.rsqrt(jnp.mean(x * x, axis=-1, keepdims=True) + EPS) * g_ref[...]
    o_ref[...] = (y * (1.0 + sc_ref[...]) + sh_ref[...]).astype(o_ref.dtype)


def _norm_mod(h, g, m, sub, ctx_len, slab):
    mrows, d = h.shape
    tr = ctx_len
    per = slab // tr

    def mrow(j):
        return jnp.where(j % per == 0, 0, 1 + j // per)

    return pl.pallas_call(
        _norm_mod_kernel,
        grid=(mrows // tr,),
        in_specs=[pl.BlockSpec((tr, d), lambda j: (j, 0)),
                  pl.BlockSpec((1, d), lambda j: (0, 0)),
                  pl.BlockSpec((None, 1, d), lambda j: (mrow(j), 0, 3 * sub)),
                  pl.BlockSpec((None, 1, d), lambda j: (mrow(j), 0, 3 * sub + 1))],
        out_specs=pl.BlockSpec((tr, d), lambda j: (j, 0)),
        out_shape=jax.ShapeDtypeStruct((mrows, d), BF16),
        compiler_params=_cparams(1),
        name="norm_mod",
    )(h, g, m, m)


def _final_norm_kernel(h_ref, g_ref, o_ref):
    x = h_ref[...]
    o_ref[...] = x * lax.rsqrt(jnp.mean(x * x, axis=-1, keepdims=True) + EPS) * g_ref[...]


def _final_norm(h, g, bsz, ctx_len, seq):
    d = h.shape[-1]
    slab = ctx_len + seq
    tr = ctx_len
    return pl.pallas_call(
        _final_norm_kernel,
        grid=(bsz, seq // tr),
        in_specs=[pl.BlockSpec((None, tr, d), lambda b, j: (b, 1 + j, 0)),
                  pl.BlockSpec((1, d), lambda b, j: (0, 0))],
        out_specs=pl.BlockSpec((None, tr, d), lambda b, j: (b, j, 0)),
        out_shape=jax.ShapeDtypeStruct((bsz, seq, d), F32),
        compiler_params=_cparams(2),
        name="final_norm",
    )(h.reshape(bsz, slab, d), g)


def _proj_kernel(a_ref, w_ref, o_ref):
    o_ref[...] = _dot(a_ref[...], w_ref[...]).astype(o_ref.dtype)


def _proj(a, w, l):
    mrows, k = a.shape
    n = w.shape[-1]
    return pl.pallas_call(
        _proj_kernel,
        grid=(mrows // TM, n // TN),
        in_specs=[pl.BlockSpec((TM, k), lambda i, j: (i, 0)),
                  pl.BlockSpec((None, k, TN), lambda i, j: (l, 0, j))],
        out_specs=pl.BlockSpec((TM, TN), lambda i, j: (i, j)),
        out_shape=jax.ShapeDtypeStruct((mrows, n), BF16),
        compiler_params=_cparams(2),
        name="in_proj",
    )(a, w)


def _swiglu_kernel(u_ref, wa_ref, wb_ref, o_ref):
    u = u_ref[...]
    a = _dot(u, wa_ref[...])
    b = _dot(u, wb_ref[...])
    o_ref[...] = (a * jax.nn.sigmoid(a) * b).astype(o_ref.dtype)


def _ffn_up(u, w_up, l, s):
    mrows, k = u.shape
    dff = w_up.shape[-1] // 2
    nb = dff // TN
    return pl.pallas_call(
        _swiglu_kernel,
        grid=(mrows // TM, nb),
        in_specs=[pl.BlockSpec((TM, k), lambda i, j: (i, 0)),
                  pl.BlockSpec((None, None, k, TN), lambda i, j: (l, s, 0, j)),
                  pl.BlockSpec((None, None, k, TN), lambda i, j: (l, s, 0, nb + j))],
        out_specs=pl.BlockSpec((TM, TN), lambda i, j: (i, j)),
        out_shape=jax.ShapeDtypeStruct((mrows, dff), BF16),
        compiler_params=_cparams(2),
        name="ffn_up",
    )(u, w_up, w_up)


def _resid_kernel(a_ref, w_ref, h_ref, gc_ref, gx_ref, o_ref, *, scale, ctx_len, slab):
    tm = a_ref.shape[0]
    acc = _dot(a_ref[...], w_ref[...])
    row0 = (pl.program_id(0) * tm) % slab
    rows = lax.broadcasted_iota(jnp.int32, (tm, 1), 0) + row0
    gate = jnp.where(rows < ctx_len, gc_ref[...], gx_ref[...])
    o_ref[...] = h_ref[...] + (scale * gate) * acc


def _resid_proj(a, w, widx, h, m, sub, scale, ctx_len, slab):
    mrows, k = a.shape
    n = w.shape[-1]
    g0 = (3 * sub + 2) * (n // TN)
    return pl.pallas_call(
        functools.partial(_resid_kernel, scale=scale, ctx_len=ctx_len, slab=slab),
        grid=(mrows // TM, n // TN),
        in_specs=[pl.BlockSpec((TM, k), lambda i, j: (i, 0)),
                  pl.BlockSpec((None, k, TN), lambda i, j: (widx, 0, j)),
                  pl.BlockSpec((TM, TN), lambda i, j: (i, j)),
                  pl.BlockSpec((None, 1, TN), lambda i, j: (0, 0, g0 + j)),
                  pl.BlockSpec((None, 1, TN), lambda i, j: (1 + (i * TM) // slab, 0, g0 + j))],
        out_specs=pl.BlockSpec((TM, TN), lambda i, j: (i, j)),
        out_shape=jax.ShapeDtypeStruct((mrows, n), F32),
        input_output_aliases={2: 0},
        compiler_params=_cparams(2),
        name="resid_proj",
    )(a, w, h, m, m)


def _branch_kernel(a_ref, b_ref, c_ref, pa_ref, pb_ref, pc_ref, ga_ref, gb_ref, gc_ref, o_ref):
    ya = _dot(a_ref[...], pa_ref[...])
    yb = _dot(b_ref[...], pb_ref[...])
    yc = _dot(c_ref[...], pc_ref[...])
    z = (jax.nn.sigmoid(ga_ref[...].astype(F32)) * ya
         + jax.nn.sigmoid(gb_ref[...].astype(F32)) * yb
         + jax.nn.sigmoid(gc_ref[...].astype(F32)) * yc)
    o_ref[...] = z.astype(o_ref.dtype)


def _branch_proj(a, b, c, pa, pb, pc, l, p):
    mrows = a.shape[0]
    d = pa.shape[-1]
    tn = TN
    g0 = P2_GA // tn
    nb = d // tn
    return pl.pallas_call(
        _branch_kernel,
        grid=(mrows // TM, nb),
        in_specs=[pl.BlockSpec((TM, a.shape[1]), lambda i, j: (i, 0)),
                  pl.BlockSpec((TM, b.shape[1]), lambda i, j: (i, 0)),
                  pl.BlockSpec((TM, c.shape[1]), lambda i, j: (i, 0)),
                  pl.BlockSpec((None, pa.shape[1], tn), lambda i, j: (l, 0, j)),
                  pl.BlockSpec((None, pb.shape[1], tn), lambda i, j: (l, 0, j)),
                  pl.BlockSpec((None, pc.shape[1], tn), lambda i, j: (l, 0, j)),
                  pl.BlockSpec((TM, tn), lambda i, j: (i, g0 + j)),
                  pl.BlockSpec((TM, tn), lambda i, j: (i, g0 + nb + j)),
                  pl.BlockSpec((TM, tn), lambda i, j: (i, g0 + 2 * nb + j))],
        out_specs=pl.BlockSpec((TM, tn), lambda i, j: (i, j)),
        out_shape=jax.ShapeDtypeStruct((mrows, d), BF16),
        compiler_params=_cparams(2),
        name="branch_proj",
    )(a, b, c, pa, pb, pc, p, p, p)


def _conv_kernel(h_ref, bg_ref, cg_ref, w_ref, o_ref, *, ctx_len):
    n = h_ref.shape[0]
    x = cg_ref[...].astype(F32) * h_ref[...].astype(F32)
    row = lax.broadcasted_iota(jnp.int32, (n, 1), 0)
    first = (row == 0) | (row == ctx_len)
    last = (row == ctx_len - 1) | (row == n - 1)
    prev = jnp.where(first, 0.0, pltpu.roll(x, 1, axis=0))
    nxt = jnp.where(last, 0.0, pltpu.roll(x, n - 1, axis=0))
    w = w_ref[...]
    y = prev * w[0:1, :] + x * w[1:2, :] + nxt * w[2:3, :]
    o_ref[...] = (bg_ref[...].astype(F32) * y).astype(o_ref.dtype)


def _conv_branch(p, conv_w, bsz, ctx_len, slab):
    tc = 256
    nb = CONV_W // tc
    return pl.pallas_call(
        functools.partial(_conv_kernel, ctx_len=ctx_len),
        grid=(bsz, nb),
        in_specs=[pl.BlockSpec((slab, tc), lambda b, j: (b, P_H // tc + j)),
                  pl.BlockSpec((slab, tc), lambda b, j: (b, P_BG // tc + j)),
                  pl.BlockSpec((slab, tc), lambda b, j: (b, P_CG // tc + j)),
                  pl.BlockSpec((conv_w.shape[0], tc), lambda b, j: (0, j))],
        out_specs=pl.BlockSpec((slab, tc), lambda b, j: (b, j)),
        out_shape=jax.ShapeDtypeStruct((bsz * slab, CONV_W), BF16),
        compiler_params=_cparams(2),
        name="conv_branch",
    )(p, p, p, conv_w)


def _log_sigmoid(x):
    return jnp.minimum(x, 0.0) - jnp.log1p(jnp.exp(-jnp.abs(x)))


def _gla_kernel(q_ref, k_ref, v_ref, g_ref, lr_ref, cos_ref, sin_ref, wdf_ref, wdb_ref,
                bdf_ref, bdb_ref, ng_ref, o_ref, qd_s, ki_s, ke_s, dec_s, oacc_s, st_s,
                *, ctx_len):
    n = q_ref.shape[0]
    nchunk = n // GLA_CHUNK
    cchunk = ctx_len // GLA_CHUNK
    half = GLA_DKP // 2
    cos = cos_ref[...]
    sin = sin_ref[...]

    def rope(x):
        x1 = x[:, :half]
        x2 = x[:, half:]
        return jnp.concatenate([x1 * cos - x2 * sin, x1 * sin + x2 * cos], axis=1)

    qr = rope(q_ref[...].astype(F32)) * (GLA_DK ** -0.5)
    kr = rope(k_ref[...].astype(F32))
    lr = lr_ref[...]
    ridx = lax.broadcasted_iota(jnp.int32, (n, 1), 0) % GLA_CHUNK
    ci = lax.broadcasted_iota(jnp.int32, (GLA_CHUNK, GLA_CHUNK), 0)
    cj = lax.broadcasted_iota(jnp.int32, (GLA_CHUNK, GLA_CHUNK), 1)

    def chunk_sums(la):
        cum = la
        for sft in (1, 2, 4, 8, 16, 32):
            cum = cum + jnp.where(ridx >= sft, pltpu.roll(cum, sft, axis=0), 0.0)
        c3 = cum.reshape(nchunk, GLA_CHUNK, GLA_DKP)
        tot3 = c3[:, GLA_CHUNK - 1:GLA_CHUNK, :]
        tot = jnp.broadcast_to(tot3, c3.shape).reshape(n, GLA_DKP)
        return cum, tot, tot3

    def stage(cum, tot, tot3):
        qd_s[...] = (qr * jnp.exp(cum)).astype(BF16)
        ki_s[...] = (kr * jnp.exp(-cum)).astype(BF16)
        ke_s[...] = (kr * jnp.exp(tot - cum)).astype(BF16)
        dec_s[...] = jnp.exp(tot3)

    def scan(chunk_of, steps, mask, first):
        def body(i, carry):
            c = chunk_of(i)
            r0 = pl.multiple_of(c * GLA_CHUNK, GLA_CHUNK)
            qd = qd_s[pl.ds(r0, GLA_CHUNK), :]
            ki = ki_s[pl.ds(r0, GLA_CHUNK), :]
            ke = ke_s[pl.ds(r0, GLA_CHUNK), :]
            vv = v_ref[pl.ds(r0, GLA_CHUNK), :]
            att = jnp.where(mask, _dot_nt(qd, ki), 0.0)
            st = st_s[...]
            o = _dot(att.astype(BF16), vv) + _dot_nt(qd, st.astype(BF16))
            if first:
                oacc_s[pl.ds(r0, GLA_CHUNK), :] = o
            else:
                oacc_s[pl.ds(r0, GLA_CHUNK), :] += o
            st_s[...] = st * dec_s[c] + _dot_tn(vv, ke)
            return carry
        lax.fori_loop(0, steps, body, 0, unroll=4)

    la = _log_sigmoid(_dot(lr, wdf_ref[...]) + bdf_ref[...]) / GLA_TAU
    cum, tot, tot3 = chunk_sums(la)
    stage(cum, tot, tot3)
    st_s[...] = jnp.zeros_like(st_s)
    scan(lambda i: i, nchunk, ci >= cj, True)

    la = _log_sigmoid(_dot(lr, wdb_ref[...]) + bdb_ref[...]) / GLA_TAU
    cum, tot, tot3 = chunk_sums(la)
    stage(tot - cum + la, tot, tot3)
    st_s[...] = jnp.zeros_like(st_s)
    scan(lambda i: cchunk - 1 - i, cchunk, ci <= cj, False)
    scan(lambda i: nchunk - 1 - i, nchunk - cchunk, ci <= cj, False)

    o = oacc_s[...]
    o = o * lax.rsqrt(jnp.mean(o * o, axis=-1, keepdims=True) + EPS) * ng_ref[...]
    g = g_ref[...].astype(F32)
    o_ref[...] = (o * (g * jax.nn.sigmoid(g))).astype(o_ref.dtype)


def _gla_branch(p, cos, sin, wdf, wdb, bdf, bdb, ng, bsz, ctx_len, slab):
    lr_col = P_LR
    nchunk = slab // GLA_CHUNK
    head = lambda b, h: (h, 0, 0)
    return pl.pallas_call(
        functools.partial(_gla_kernel, ctx_len=ctx_len),
        grid=(bsz, GLA_HEADS),
        in_specs=[pl.BlockSpec((slab, GLA_DKP), lambda b, h: (b, P_Q // GLA_DKP + h)),
                  pl.BlockSpec((slab, GLA_DKP), lambda b, h: (b, P_K // GLA_DKP + h)),
                  pl.BlockSpec((slab, GLA_DV), lambda b, h: (b, P_V // GLA_DV + h)),
                  pl.BlockSpec((slab, GLA_DV), lambda b, h: (b, P_G // GLA_DV + h)),
                  pl.BlockSpec((slab, 128), lambda b, h: (b, lr_col // 128)),
                  pl.BlockSpec((slab, GLA_DKP // 2), lambda b, h: (0, 0)),
                  pl.BlockSpec((slab, GLA_DKP // 2), lambda b, h: (0, 0)),
                  pl.BlockSpec((None, 128, GLA_DKP), head),
                  pl.BlockSpec((None, 128, GLA_DKP), head),
                  pl.BlockSpec((None, 1, GLA_DKP), head),
                  pl.BlockSpec((None, 1, GLA_DKP), head),
                  pl.BlockSpec((None, 1, GLA_DV), head)],
        out_specs=pl.BlockSpec((slab, GLA_DV), lambda b, h: (b, h)),
        out_shape=jax.ShapeDtypeStruct((bsz * slab, GLA_HEADS * GLA_DV), BF16),
        scratch_shapes=[pltpu.VMEM((slab, GLA_DKP), BF16),
                        pltpu.VMEM((slab, GLA_DKP), BF16),
                        pltpu.VMEM((slab, GLA_DKP), BF16),
                        pltpu.VMEM((nchunk, 1, GLA_DKP), F32),
                        pltpu.VMEM((slab, GLA_DV), F32),
                        pltpu.VMEM((GLA_DV, GLA_DKP), F32)],
        compiler_params=_cparams(2),
        name="gla_branch",
    )(p, p, p, p, p, cos, sin, wdf, wdb, bdf, bdb, ng)


def _na_kernel(q_ref, k_ref, v_ref, bias_ref, o_ref, *, ctx_len, rows):
    scale = NA_HD ** -0.5
    gq = NA_GROUP * GRID_W
    nwin = NA_WIN * GRID_W
    kc = k_ref[0:ctx_len, :]
    vc = v_ref[0:ctx_len, :]

    s = _dot_nt(q_ref[0:ctx_len, :], kc) * scale
    e = jnp.exp(s - jnp.max(s, axis=-1, keepdims=True))
    pr = e / jnp.sum(e, axis=-1, keepdims=True)
    o_ref[0:ctx_len, :] = _dot(pr.astype(BF16), vc).astype(o_ref.dtype)

    ngroups = rows // NA_GROUP
    for g in range(ngroups):
        kind = 0 if g == 0 else (2 if g == ngroups - 1 else 1)
        q0 = ctx_len + g * gq
        k0 = ctx_len + min(max(g * NA_GROUP - NA_KR // 2, 0), rows - NA_WIN) * GRID_W
        qg = q_ref[q0:q0 + gq, :]
        s_loc = _dot_nt(qg, k_ref[k0:k0 + nwin, :]) * scale + bias_ref[kind]
        s_ctx = _dot_nt(qg, kc) * scale
        m = jnp.maximum(jnp.max(s_loc, axis=-1, keepdims=True), jnp.max(s_ctx, axis=-1, keepdims=True))
        e_loc = jnp.exp(s_loc - m)
        e_ctx = jnp.exp(s_ctx - m)
        inv = 1.0 / (jnp.sum(e_loc, axis=-1, keepdims=True) + jnp.sum(e_ctx, axis=-1, keepdims=True))
        o = _dot((e_loc * inv).astype(BF16), v_ref[k0:k0 + nwin, :]) + _dot((e_ctx * inv).astype(BF16), vc)
        o_ref[q0:q0 + gq, :] = o.astype(o_ref.dtype)


def _na_branch(p, bias, l, bsz, ctx_len, slab):
    rows = (slab - ctx_len) // GRID_W
    return pl.pallas_call(
        functools.partial(_na_kernel, ctx_len=ctx_len, rows=rows),
        grid=(NA_HEADS, bsz),
        in_specs=[pl.BlockSpec((slab, NA_HD), lambda h, b: (b, P2_NQ // NA_HD + h)),
                  pl.BlockSpec((slab, NA_HD), lambda h, b: (b, P2_NK // NA_HD + h)),
                  pl.BlockSpec((slab, NA_HD), lambda h, b: (b, P2_NV // NA_HD + h)),
                  pl.BlockSpec((None, None) + bias.shape[2:], lambda h, b: (l, h, 0, 0, 0))],
        out_specs=pl.BlockSpec((slab, NA_HD), lambda h, b: (b, h)),
        out_shape=jax.ShapeDtypeStruct((bsz * slab, NA_W), BF16),
        compiler_params=_cparams(2),
        name="na_branch",
    )(p, p, p, bias)


def _gla_dk_order():
    q = GLA_DK // 4
    one = np.full((GLA_DKP,), -1, np.int64)
    one[0:q] = np.arange(0, q)
    one[q:2 * q] = np.arange(2 * q, 3 * q)
    one[GLA_DKP // 2:GLA_DKP // 2 + q] = np.arange(q, 2 * q)
    one[GLA_DKP // 2 + q:GLA_DKP // 2 + 2 * q] = np.arange(3 * q, 4 * q)
    out = np.concatenate([np.where(one >= 0, one + h * GLA_DK, -1) for h in range(GLA_HEADS)])
    return out


def _take_cols(w, cols):
    pieces = []
    i = 0
    n = len(cols)
    while i < n:
        j = i + 1
        if cols[i] < 0:
            while j < n and cols[j] < 0:
                j += 1
            pieces.append(jnp.zeros(w.shape[:-1] + (j - i,), w.dtype))
        else:
            while j < n and cols[j] == cols[j - 1] + 1:
                j += 1
            pieces.append(w[..., int(cols[i]):int(cols[i]) + (j - i)])
        i = j
    return jnp.concatenate(pieces, axis=-1)


def _split_in_proj(w_in):
    o_q = 3 * CONV_W
    o_k = o_q + GLA_HEADS * GLA_DK
    o_v = o_k + GLA_HEADS * GLA_DK
    o_lr = o_v + 2 * GLA_HEADS * GLA_DV
    o_nq = o_lr + 2 * GLA_RANK
    order = _gla_dk_order()
    lead = w_in.shape[:-1]
    w1 = jnp.concatenate([
        w_in[..., 0:o_q].astype(BF16),
        w_in[..., o_v:o_lr].astype(BF16),
        _take_cols(w_in[..., o_q:o_k], order).astype(BF16),
        _take_cols(w_in[..., o_k:o_v], order).astype(BF16),
        w_in[..., o_lr:o_nq].astype(BF16),
        jnp.zeros(lead + (LR_PAD - 2 * GLA_RANK,), BF16)], axis=-1)
    w2 = w_in[..., o_nq:].astype(BF16)
    return w1, w2


def _rope_tables(ctx_len, seq):
    q = GLA_DK // 4
    pos = jnp.arange(seq)
    freq = ROPE_THETA ** (-jnp.arange(q, dtype=F32) / q)
    ang_r = (pos // GRID_W).astype(F32)[:, None] * freq[None, :]
    ang_c = (pos % GRID_W).astype(F32)[:, None] * freq[None, :]
    pad = GLA_DKP // 2 - 2 * q
    cos = jnp.concatenate([jnp.cos(ang_r), jnp.cos(ang_c), jnp.ones((seq, pad), F32)], axis=1)
    sin = jnp.concatenate([jnp.sin(ang_r), jnp.sin(ang_c), jnp.zeros((seq, pad), F32)], axis=1)
    cos = jnp.concatenate([jnp.ones((ctx_len, GLA_DKP // 2), F32), cos], axis=0)
    sin = jnp.concatenate([jnp.zeros((ctx_len, GLA_DKP // 2), F32), sin], axis=0)
    return cos, sin


def _na_bias_tables(rpb, rows):
    ngroups = rows // NA_GROUP
    depth, nh, nro, _ = rpb.shape
    q_c = np.arange(GRID_W)
    k_c = np.arange(GRID_W)
    win_c = np.clip(q_c - NA_KC // 2, 0, GRID_W - NA_KC)
    col_ok = (k_c[None, :] >= win_c[:, None]) & (k_c[None, :] < win_c[:, None] + NA_KC)
    col_off = np.clip(k_c[None, :] - q_c[:, None] + NA_KC - 1, 0, 2 * NA_KC - 2)
    blocks = jnp.where(jnp.asarray(col_ok), rpb[:, :, :, col_off].astype(F32), -jnp.inf)
    blocks = jnp.concatenate([blocks, jnp.full((depth, nh, 1, GRID_W, GRID_W), -jnp.inf, F32)], axis=2)
    tabs = []
    q_r = np.arange(NA_GROUP)
    k_r = np.arange(NA_WIN)
    for g in (0, 1, ngroups - 1):
        base = min(max(g * NA_GROUP - NA_KR // 2, 0), rows - NA_WIN)
        r_abs = g * NA_GROUP + q_r
        win_r = np.clip(r_abs - NA_KR // 2, 0, rows - NA_KR)
        key_r = base + k_r
        row_ok = (key_r[None, :] >= win_r[:, None]) & (key_r[None, :] < win_r[:, None] + NA_KR)
        row_off = np.where(row_ok, key_r[None, :] - r_abs[:, None] + NA_KR - 1, nro)
        t = blocks[:, :, row_off]
        tabs.append(t.transpose(0, 1, 2, 4, 3, 5).reshape(depth, nh, NA_GROUP * GRID_W, NA_WIN * GRID_W))
    return jnp.stack(tabs, axis=2)


def kernel(x, c, ctx, c_ctx, mod_a, mod_b, mod_bias, norm_g, ffn_up, ffn_down, w_in, conv_w,
           gla_decay_w, gla_decay_b, gla_norm_g, na_rpb, w_branch_conv, w_branch_gla, w_branch_na,
           w_out, final_g):
    bsz, seq, d = x.shape
    ctx_len = ctx.shape[1]
    depth = mod_a.shape[0]
    slab = ctx_len + seq
    rows = seq // GRID_W
    assert slab % TM == 0 and d % TN == 0 and seq % ctx_len == 0
    assert rows % NA_GROUP == 0 and rows >= NA_WIN and ctx_len % GLA_CHUNK == 0

    mrows = 16
    cvec = jnp.concatenate([c_ctx[None, :], c, jnp.zeros((mrows - 1 - bsz, d), F32)], axis=0)
    mods = _modulation(cvec, mod_a, mod_b, mod_bias)[:, :1 + bsz].reshape(depth, 1 + bsz, 1, N_MOD * d)

    w1, w2 = _split_in_proj(w_in)
    ffn_up_b = ffn_up.astype(BF16)
    ffn_down_b = ffn_down.astype(BF16).reshape(depth * 2, ffn_down.shape[2], d)
    wb_conv = w_branch_conv.astype(BF16)
    wb_gla = w_branch_gla.astype(BF16)
    wb_na = w_branch_na.astype(BF16)
    w_out_b = w_out.astype(BF16)
    order = _gla_dk_order()
    dw = _take_cols(gla_decay_w, order).reshape(depth, 2, GLA_RANK, GLA_HEADS, GLA_DKP)
    dw = dw.transpose(0, 1, 3, 2, 4)
    wdf = jnp.zeros((depth, GLA_HEADS, 128, GLA_DKP), F32).at[:, :, 0:GLA_RANK].set(dw[:, 0])
    wdb = jnp.zeros((depth, GLA_HEADS, 128, GLA_DKP), F32).at[:, :, GLA_RANK:2 * GLA_RANK].set(dw[:, 1])
    wdf = wdf.astype(BF16)
    wdb = wdb.astype(BF16)
    db = _take_cols(gla_decay_b, order).reshape(depth, 2, GLA_HEADS, 1, GLA_DKP)
    ng = gla_norm_g.reshape(depth, GLA_HEADS, 1, GLA_DV)
    cos, sin = _rope_tables(ctx_len, seq)
    na_bias = _na_bias_tables(na_rpb, rows)

    h = jnp.concatenate([ctx, x], axis=1).reshape(bsz * slab, d)
    for l in range(depth):
        m = mods[l]
        nm = functools.partial(_norm_mod, m=m, ctx_len=ctx_len, slab=slab)
        rp = functools.partial(_resid_proj, m=m, ctx_len=ctx_len, slab=slab)
        act = _ffn_up(nm(h, norm_g[l, 0:1], sub=0), ffn_up_b, l, 0)
        h = rp(act, ffn_down_b, 2 * l, h, sub=0, scale=0.5)
        u = nm(h, norm_g[l, 1:2], sub=1)
        p1 = _proj(u, w1, l)
        p2 = _proj(u, w2, l)
        ya = _conv_branch(p1, conv_w[l], bsz, ctx_len, slab)
        yb = _gla_branch(p1, cos, sin, wdf[l], wdb[l], db[l, 0], db[l, 1], ng[l], bsz, ctx_len, slab)
        yc = _na_branch(p2, na_bias, l, bsz, ctx_len, slab)
        z = _branch_proj(ya, yb, yc, wb_conv, wb_gla, wb_na, l, p2)
        h = rp(z, w_out_b, l, h, sub=1, scale=1.0)
        act = _ffn_up(nm(h, norm_g[l, 2:3], sub=2), ffn_up_b, l, 1)
        h = rp(act, ffn_down_b, 2 * l + 1, h, sub=2, scale=0.5)
    return _final_norm(h, final_g.reshape(1, d), bsz, ctx_len, seq)
```

```python
import functools

import numpy as np
import jax
import jax.numpy as jnp
from jax import lax
from jax.experimental import pallas as pl
from jax.experimental.pallas import tpu as pltpu

F32 = jnp.float32
BF16 = jnp.bfloat16

GRID_W = 64
N_MOD = 9
CONV_W = 1024
GLA_HEADS = 4
GLA_DK = 192
GLA_DV = 384
GLA_DKP = 256
GLA_RANK = 16
GLA_TAU = 16.0
GLA_CHUNK = 64
NA_HEADS = 12
NA_HD = 128
NA_W = NA_HEADS * NA_HD
NA_KR = 8
NA_KC = 16
NA_GROUP = 4
NA_WIN = NA_KR + NA_GROUP - 1
ROPE_THETA = 10000.0
EPS = 1e-6

P_H, P_BG, P_CG = 0, 1024, 2048
P_V, P_G = 3072, 4608
P1_W = 6144
PQ_Q, PQ_K = 0, 1024
LR_LANE = 96
P2_NQ, P2_NK, P2_NV = 0, 1536, 3072
P2_GA = 4608

VMEM_LIMIT = 56 * 1024 * 1024
TM = 1152
TN = 512


def _cparams(n_axes):
    return pltpu.CompilerParams(dimension_semantics=("arbitrary",) * n_axes,
                                vmem_limit_bytes=VMEM_LIMIT)


def _dot(a, b):
    return jnp.dot(a, b, preferred_element_type=F32)


def _dot_nt(a, b):
    return lax.dot_general(a, b, (((1,), (1,)), ((), ())), preferred_element_type=F32)


def _dot_tn(a, b):
    return lax.dot_general(a, b, (((0,), (0,)), ((), ())), preferred_element_type=F32)


def _mod_a_kernel(c_ref, a_ref, o_ref):
    c = c_ref[...]
    s = (c * jax.nn.sigmoid(c)).astype(BF16)
    o_ref[...] = _dot(s, a_ref[...].astype(BF16))


def _mod_b_kernel(t_ref, b_ref, bias_ref, o_ref):
    o_ref[...] = _dot(t_ref[...].astype(BF16), b_ref[...].astype(BF16)) + bias_ref[...]


def _modulation(cvec, mod_a, mod_b, mod_bias):
    depth, d, rank = mod_a.shape
    rows = cvec.shape[0]
    nm = mod_b.shape[-1]
    tn = min(d, 2048)
    assert nm % tn == 0
    t = pl.pallas_call(
        _mod_a_kernel,
        grid=(depth,),
        in_specs=[pl.BlockSpec((rows, d), lambda l: (0, 0)),
                  pl.BlockSpec((None, d, rank), lambda l: (l, 0, 0))],
        out_specs=pl.BlockSpec((None, rows, rank), lambda l: (l, 0, 0)),
        out_shape=jax.ShapeDtypeStruct((depth, rows, rank), F32),
        compiler_params=_cparams(1),
        name="mod_a",
    )(cvec, mod_a)
    return pl.pallas_call(
        _mod_b_kernel,
        grid=(depth, nm // tn),
        in_specs=[pl.BlockSpec((None, rows, rank), lambda l, j: (l, 0, 0)),
                  pl.BlockSpec((None, rank, tn), lambda l, j: (l, 0, j)),
                  pl.BlockSpec((None, 1, tn), lambda l, j: (l, 0, j))],
        out_specs=pl.BlockSpec((None, rows, tn), lambda l, j: (l, 0, j)),
        out_shape=jax.ShapeDtypeStruct((depth, rows, nm), F32),
        compiler_params=_cparams(2),
        name="mod_b",
    )(t, mod_b, mod_bias.reshape(depth, 1, nm))


def _norm_mod_kernel(h_ref, g_ref, sh_ref, sc_ref, o_ref):
    x = h_ref[...]
    y = x * lax.rsqrt(jnp.mean(x * x, axis=-1, keepdims=True) + EPS) * g_ref[...]
    o_ref[...] = (y * (1.0 + sc_ref[...]) + sh_ref[...]).astype(o_ref.dtype)


def _norm_mod(h, g, m, sub, ctx_len, slab):
    mrows, d = h.shape
    tr = ctx_len
    per = slab // tr

    def mrow(j):
        return jnp.where(j % per == 0, 0, 1 + j // per)

    return pl.pallas_call(
        _norm_mod_kernel,
        grid=(mrows // tr,),
        in_specs=[pl.BlockSpec((tr, d), lambda j: (j, 0)),
                  pl.BlockSpec((1, d), lambda j: (0, 0)),
                  pl.BlockSpec((None, 1, d), lambda j: (mrow(j), 0, 3 * sub)),
                  pl.BlockSpec((None, 1, d), lambda j: (mrow(j), 0, 3 * sub + 1))],
        out_specs=pl.BlockSpec((tr, d), lambda j: (j, 0)),
        out_shape=jax.ShapeDtypeStruct((mrows, d), BF16),
        compiler_params=_cparams(1),
        name="norm_mod",
    )(h, g, m, m)


def _final_norm_kernel(h_ref, g_ref, o_ref):
    x = h_ref[...]
    o_ref[...] = x * lax.rsqrt(jnp.mean(x * x, axis=-1, keepdims=True) + EPS) * g_ref[...]


def _final_norm(h, g, bsz, ctx_len, seq):
    d = h.shape[-1]
    slab = ctx_len + seq
    tr = ctx_len
    return pl.pallas_call(
        _final_norm_kernel,
        grid=(bsz, seq // tr),
        in_specs=[pl.BlockSpec((None, tr, d), lambda b, j: (b, 1 + j, 0)),
                  pl.BlockSpec((1, d), lambda b, j: (0, 0))],
        out_specs=pl.BlockSpec((None, tr, d), lambda b, j: (b, j, 0)),
        out_shape=jax.ShapeDtypeStruct((bsz, seq, d), F32),
        compiler_params=_cparams(2),
        name="final_norm",
    )(h.reshape(bsz, slab, d), g)


def _proj_kernel(a_ref, w_ref, o_ref):
    o_ref[...] = _dot(a_ref[...], w_ref[...].astype(BF16)).astype(o_ref.dtype)


def _proj_runs(a, w, l, runs):
    mrows, k = a.shape
    assert all(s % TN == 0 and n % TN == 0 for s, n in runs)
    src = np.concatenate([np.arange(s // TN, (s + n) // TN) for s, n in runs])
    first, step_at = int(src[0]), [int(i) for i in np.nonzero(np.diff(src) != 1)[0] + 1]
    jumps = [int(src[i] - src[i - 1] - 1) for i in step_at]

    def wcol(j):
        c = j + first
        for at, jump in zip(step_at, jumps):
            c = c + jnp.where(j >= at, jump, 0)
        return c

    return pl.pallas_call(
        _proj_kernel,
        grid=(mrows // TM, len(src)),
        in_specs=[pl.BlockSpec((TM, k), lambda i, j: (i, 0)),
                  pl.BlockSpec((None, k, TN), lambda i, j: (l, 0, wcol(j)))],
        out_specs=pl.BlockSpec((TM, TN), lambda i, j: (i, j)),
        out_shape=jax.ShapeDtypeStruct((mrows, len(src) * TN), BF16),
        compiler_params=_cparams(2),
        name="in_proj_runs",
    )(a, w)


def _proj(a, w, l):
    mrows, k = a.shape
    n = w.shape[-1]
    return pl.pallas_call(
        _proj_kernel,
        grid=(mrows // TM, n // TN),
        in_specs=[pl.BlockSpec((TM, k), lambda i, j: (i, 0)),
                  pl.BlockSpec((None, k, TN), lambda i, j: (l, 0, j))],
        out_specs=pl.BlockSpec((TM, TN), lambda i, j: (i, j)),
        out_shape=jax.ShapeDtypeStruct((mrows, n), BF16),
        compiler_params=_cparams(2),
        name="in_proj",
    )(a, w)


def _swiglu_kernel(u_ref, wa_ref, wb_ref, o_ref):
    u = u_ref[...]
    a = _dot(u, wa_ref[...])
    b = _dot(u, wb_ref[...])
    o_ref[...] = (a * jax.nn.sigmoid(a) * b).astype(o_ref.dtype)


def _ffn_up(u, w_up, l, s):
    mrows, k = u.shape
    dff = w_up.shape[-1] // 2
    nb = dff // TN
    return pl.pallas_call(
        _swiglu_kernel,
        grid=(mrows // TM, nb),
        in_specs=[pl.BlockSpec((TM, k), lambda i, j: (i, 0)),
                  pl.BlockSpec((None, None, k, TN), lambda i, j: (l, s, 0, j)),
                  pl.BlockSpec((None, None, k, TN), lambda i, j: (l, s, 0, nb + j))],
        out_specs=pl.BlockSpec((TM, TN), lambda i, j: (i, j)),
        out_shape=jax.ShapeDtypeStruct((mrows, dff), BF16),
        compiler_params=_cparams(2),
        name="ffn_up",
    )(u, w_up, w_up)


def _resid_kernel(a_ref, w_ref, h_ref, gc_ref, gx_ref, o_ref, *, scale, ctx_len, slab):
    tm = a_ref.shape[0]
    acc = _dot(a_ref[...], w_ref[...])
    row0 = (pl.program_id(0) * tm) % slab
    rows = lax.broadcasted_iota(jnp.int32, (tm, 1), 0) + row0
    gate = jnp.where(rows < ctx_len, gc_ref[...], gx_ref[...])
    o_ref[...] = h_ref[...] + (scale * gate) * acc


def _resid_proj(a, w, widx, h, m, sub, scale, ctx_len, slab):
    mrows, k = a.shape
    n = w.shape[-1]
    g0 = (3 * sub + 2) * (n // TN)
    return pl.pallas_call(
        functools.partial(_resid_kernel, scale=scale, ctx_len=ctx_len, slab=slab),
        grid=(mrows // TM, n // TN),
        in_specs=[pl.BlockSpec((TM, k), lambda i, j: (i, 0)),
                  pl.BlockSpec((None, k, TN), lambda i, j: (widx, 0, j)),
                  pl.BlockSpec((TM, TN), lambda i, j: (i, j)),
                  pl.BlockSpec((None, 1, TN), lambda i, j: (0, 0, g0 + j)),
                  pl.BlockSpec((None, 1, TN), lambda i, j: (1 + (i * TM) // slab, 0, g0 + j))],
        out_specs=pl.BlockSpec((TM, TN), lambda i, j: (i, j)),
        out_shape=jax.ShapeDtypeStruct((mrows, n), F32),
        input_output_aliases={2: 0},
        compiler_params=_cparams(2),
        name="resid_proj",
    )(a, w, h, m, m)


def _branch_kernel(a_ref, b_ref, c_ref, pa_ref, pb_ref, pc_ref, ga_ref, gb_ref, gc_ref, o_ref):
    ya = _dot(a_ref[...], pa_ref[...])
    yb = _dot(b_ref[...], pb_ref[...])
    yc = _dot(c_ref[...], pc_ref[...])
    z = (jax.nn.sigmoid(ga_ref[...].astype(F32)) * ya
         + jax.nn.sigmoid(gb_ref[...].astype(F32)) * yb
         + jax.nn.sigmoid(gc_ref[...].astype(F32)) * yc)
    o_ref[...] = z.astype(o_ref.dtype)


def _branch_proj(a, b, c, pa, pb, pc, l, p):
    mrows = a.shape[0]
    d = pa.shape[-1]
    tn = TN
    g0 = P2_GA // tn
    nb = d // tn
    return pl.pallas_call(
        _branch_kernel,
        grid=(mrows // TM, nb),
        in_specs=[pl.BlockSpec((TM, a.shape[1]), lambda i, j: (i, 0)),
                  pl.BlockSpec((TM, b.shape[1]), lambda i, j: (i, 0)),
                  pl.BlockSpec((TM, c.shape[1]), lambda i, j: (i, 0)),
                  pl.BlockSpec((None, pa.shape[1], tn), lambda i, j: (l, 0, j)),
                  pl.BlockSpec((None, pb.shape[1], tn), lambda i, j: (l, 0, j)),
                  pl.BlockSpec((None, pc.shape[1], tn), lambda i, j: (l, 0, j)),
                  pl.BlockSpec((TM, tn), lambda i, j: (i, g0 + j)),
                  pl.BlockSpec((TM, tn), lambda i, j: (i, g0 + nb + j)),
                  pl.BlockSpec((TM, tn), lambda i, j: (i, g0 + 2 * nb + j))],
        out_specs=pl.BlockSpec((TM, tn), lambda i, j: (i, j)),
        out_shape=jax.ShapeDtypeStruct((mrows, d), BF16),
        compiler_params=_cparams(2),
        name="branch_proj",
    )(a, b, c, pa, pb, pc, p, p, p)


def _conv_kernel(h_ref, bg_ref, cg_ref, w_ref, o_ref, *, ctx_len):
    n = h_ref.shape[0]
    x = cg_ref[...].astype(F32) * h_ref[...].astype(F32)
    row = lax.broadcasted_iota(jnp.int32, (n, 1), 0)
    first = (row == 0) | (row == ctx_len)
    last = (row == ctx_len - 1) | (row == n - 1)
    prev = jnp.where(first, 0.0, pltpu.roll(x, 1, axis=0))
    nxt = jnp.where(last, 0.0, pltpu.roll(x, n - 1, axis=0))
    w = w_ref[...]
    y = prev * w[0:1, :] + x * w[1:2, :] + nxt * w[2:3, :]
    o_ref[...] = (bg_ref[...].astype(F32) * y).astype(o_ref.dtype)


def _conv_branch(p, conv_w, bsz, ctx_len, slab):
    tc = 256
    nb = CONV_W // tc
    return pl.pallas_call(
        functools.partial(_conv_kernel, ctx_len=ctx_len),
        grid=(bsz, nb),
        in_specs=[pl.BlockSpec((slab, tc), lambda b, j: (b, P_H // tc + j)),
                  pl.BlockSpec((slab, tc), lambda b, j: (b, P_BG // tc + j)),
                  pl.BlockSpec((slab, tc), lambda b, j: (b, P_CG // tc + j)),
                  pl.BlockSpec((conv_w.shape[0], tc), lambda b, j: (0, j))],
        out_specs=pl.BlockSpec((slab, tc), lambda b, j: (b, j)),
        out_shape=jax.ShapeDtypeStruct((bsz * slab, CONV_W), BF16),
        compiler_params=_cparams(2),
        name="conv_branch",
    )(p, p, p, conv_w)


def _log_sigmoid(x):
    return jnp.minimum(x, 0.0) - jnp.log(1.0 + jnp.exp(-jnp.abs(x)))


def _gla_kernel(q_ref, k_ref, v_ref, g_ref, lr_ref, cos_ref, sin_ref, wdf_ref, wdb_ref,
                bdf_ref, bdb_ref, ng_ref, o_ref, qd_s, ki_s, ke_s, dec_s, oacc_s, st_s,
                *, ctx_len):
    n = q_ref.shape[0]
    nchunk = n // GLA_CHUNK
    cchunk = ctx_len // GLA_CHUNK
    half = GLA_DKP // 2
    cos = cos_ref[...]
    sin = sin_ref[...]

    def rope(x):
        x1 = x[:, :half]
        x2 = x[:, half:]
        return jnp.concatenate([x1 * cos - x2 * sin, x1 * sin + x2 * cos], axis=1)

    qr = rope(q_ref[...].astype(F32)) * (GLA_DK ** -0.5)
    kr = rope(k_ref[...].astype(F32))
    lr = lr_ref[...]
    ridx = lax.broadcasted_iota(jnp.int32, (n, 1), 0) % GLA_CHUNK
    ci = lax.broadcasted_iota(jnp.int32, (GLA_CHUNK, GLA_CHUNK), 0)
    cj = lax.broadcasted_iota(jnp.int32, (GLA_CHUNK, GLA_CHUNK), 1)

    def chunk_sums(la):
        cum = la
        for sft in (1, 2, 4, 8, 16, 32):
            cum = cum + jnp.where(ridx >= sft, pltpu.roll(cum, sft, axis=0), 0.0)
        c3 = cum.reshape(nchunk, GLA_CHUNK, GLA_DKP)
        tot3 = c3[:, GLA_CHUNK - 1:GLA_CHUNK, :]
        tot = jnp.broadcast_to(tot3, c3.shape).reshape(n, GLA_DKP)
        return cum, tot, tot3

    def stage(cum, tot, tot3):
        qd_s[...] = (qr * jnp.exp(cum)).astype(BF16)
        ki_s[...] = (kr * jnp.exp(-cum)).astype(BF16)
        ke_s[...] = (kr * jnp.exp(tot - cum)).astype(BF16)
        dec_s[...] = jnp.exp(tot3)

    def scan(chunk_of, steps, mask, first):
        def body(i, carry):
            c = chunk_of(i)
            r0 = pl.multiple_of(c * GLA_CHUNK, GLA_CHUNK)
            qd = qd_s[pl.ds(r0, GLA_CHUNK), :]
            ki = ki_s[pl.ds(r0, GLA_CHUNK), :]
            ke = ke_s[pl.ds(r0, GLA_CHUNK), :]
            vv = v_ref[pl.ds(r0, GLA_CHUNK), :]
            att = jnp.where(mask, _dot_nt(qd, ki), 0.0)
            st = st_s[...]
            o = _dot(att.astype(BF16), vv) + _dot_nt(qd, st.astype(BF16))
            if first:
                oacc_s[pl.ds(r0, GLA_CHUNK), :] = o
            else:
                oacc_s[pl.ds(r0, GLA_CHUNK), :] += o
            st_s[...] = st * dec_s[c] + _dot_tn(vv, ke)
            return carry
        lax.fori_loop(0, steps, body, 0, unroll=4)

    la = _log_sigmoid(_dot(lr, wdf_ref[...]) + bdf_ref[...]) / GLA_TAU
    cum, tot, tot3 = chunk_sums(la)
    stage(cum, tot, tot3)
    st_s[...] = jnp.zeros_like(st_s)
    scan(lambda i: i, nchunk, ci >= cj, True)

    la = _log_sigmoid(_dot(lr, wdb_ref[...]) + bdb_ref[...]) / GLA_TAU
    cum, tot, tot3 = chunk_sums(la)
    stage(tot - cum + la, tot, tot3)
    st_s[...] = jnp.zeros_like(st_s)
    scan(lambda i: cchunk - 1 - i, cchunk, ci <= cj, False)
    scan(lambda i: nchunk - 1 - i, nchunk - cchunk, ci <= cj, False)

    o = oacc_s[...]
    o = o * lax.rsqrt(jnp.mean(o * o, axis=-1, keepdims=True) + EPS) * ng_ref[...]
    g = g_ref[...].astype(F32)
    o_ref[...] = (o * (g * jax.nn.sigmoid(g))).astype(o_ref.dtype)


def _gla_branch(p1, pq, cos, sin, wdf, wdb, bdf, bdb, ng, bsz, ctx_len, slab):
    nchunk = slab // GLA_CHUNK
    head = lambda b, h: (h, 0, 0)
    return pl.pallas_call(
        functools.partial(_gla_kernel, ctx_len=ctx_len),
        grid=(bsz, GLA_HEADS),
        in_specs=[pl.BlockSpec((slab, GLA_DKP), lambda b, h: (b, PQ_Q // GLA_DKP + h)),
                  pl.BlockSpec((slab, GLA_DKP), lambda b, h: (b, PQ_K // GLA_DKP + h)),
                  pl.BlockSpec((slab, GLA_DV), lambda b, h: (b, P_V // GLA_DV + h)),
                  pl.BlockSpec((slab, GLA_DV), lambda b, h: (b, P_G // GLA_DV + h)),
                  pl.BlockSpec((slab, 128), lambda b, h: (b, PQ_Q // 128)),
                  pl.BlockSpec((slab, GLA_DKP // 2), lambda b, h: (0, 0)),
                  pl.BlockSpec((slab, GLA_DKP // 2), lambda b, h: (0, 0)),
                  pl.BlockSpec((None, 128, GLA_DKP), head),
                  pl.BlockSpec((None, 128, GLA_DKP), head),
                  pl.BlockSpec((None, 1, GLA_DKP), head),
                  pl.BlockSpec((None, 1, GLA_DKP), head),
                  pl.BlockSpec((None, 1, GLA_DV), head)],
        out_specs=pl.BlockSpec((slab, GLA_DV), lambda b, h: (b, h)),
        out_shape=jax.ShapeDtypeStruct((bsz * slab, GLA_HEADS * GLA_DV), BF16),
        scratch_shapes=[pltpu.VMEM((slab, GLA_DKP), BF16),
                        pltpu.VMEM((slab, GLA_DKP), BF16),
                        pltpu.VMEM((slab, GLA_DKP), BF16),
                        pltpu.VMEM((nchunk, 1, GLA_DKP), F32),
                        pltpu.VMEM((slab, GLA_DV), F32),
                        pltpu.VMEM((GLA_DV, GLA_DKP), F32)],
        compiler_params=_cparams(2),
        name="gla_branch",
    )(pq, pq, p1, p1, pq, cos, sin, wdf, wdb, bdf, bdb, ng)


def _na_kernel(q_ref, k_ref, v_ref, bias_ref, o_ref, *, ctx_len, rows):
    scale = NA_HD ** -0.5
    gq = NA_GROUP * GRID_W
    nwin = NA_WIN * GRID_W
    kc = k_ref[0:ctx_len, :]
    vc = v_ref[0:ctx_len, :]

    s = _dot_nt(q_ref[0:ctx_len, :], kc) * scale
    e = jnp.exp(s - jnp.max(s, axis=-1, keepdims=True))
    pr = e / jnp.sum(e, axis=-1, keepdims=True)
    o_ref[0:ctx_len, :] = _dot(pr.astype(BF16), vc).astype(o_ref.dtype)

    ngroups = rows // NA_GROUP
    for g in range(ngroups):
        kind = 0 if g == 0 else (2 if g == ngroups - 1 else 1)
        q0 = ctx_len + g * gq
        k0 = ctx_len + min(max(g * NA_GROUP - NA_KR // 2, 0), rows - NA_WIN) * GRID_W
        qg = q_ref[q0:q0 + gq, :]
        s_loc = _dot_nt(qg, k_ref[k0:k0 + nwin, :]) * scale + bias_ref[kind]
        s_ctx = _dot_nt(qg, kc) * scale
        m = jnp.maximum(jnp.max(s_loc, axis=-1, keepdims=True), jnp.max(s_ctx, axis=-1, keepdims=True))
        e_loc = jnp.exp(s_loc - m)
        e_ctx = jnp.exp(s_ctx - m)
        den = jnp.sum(e_loc, axis=-1, keepdims=True) + jnp.sum(e_ctx, axis=-1, keepdims=True)
        o = _dot(e_loc.astype(BF16), v_ref[k0:k0 + nwin, :]) + _dot(e_ctx.astype(BF16), vc)
        o_ref[q0:q0 + gq, :] = (o / den).astype(o_ref.dtype)


def _na_branch(p, bias, l, bsz, ctx_len, slab):
    rows = (slab - ctx_len) // GRID_W
    return pl.pallas_call(
        functools.partial(_na_kernel, ctx_len=ctx_len, rows=rows),
        grid=(NA_HEADS, bsz),
        in_specs=[pl.BlockSpec((slab, NA_HD), lambda h, b: (b, P2_NQ // NA_HD + h)),
                  pl.BlockSpec((slab, NA_HD), lambda h, b: (b, P2_NK // NA_HD + h)),
                  pl.BlockSpec((slab, NA_HD), lambda h, b: (b, P2_NV // NA_HD + h)),
                  pl.BlockSpec((None, None) + bias.shape[2:], lambda h, b: (l, h, 0, 0, 0))],
        out_specs=pl.BlockSpec((slab, NA_HD), lambda h, b: (b, h)),
        out_shape=jax.ShapeDtypeStruct((bsz * slab, NA_W), BF16),
        compiler_params=_cparams(2),
        name="na_branch",
    )(p, p, p, bias)


def _gla_dk_order():
    q = GLA_DK // 4
    one = np.full((GLA_DKP,), -1, np.int64)
    one[0:q] = np.arange(0, q)
    one[q:2 * q] = np.arange(2 * q, 3 * q)
    one[GLA_DKP // 2:GLA_DKP // 2 + q] = np.arange(q, 2 * q)
    one[GLA_DKP // 2 + q:GLA_DKP // 2 + 2 * q] = np.arange(3 * q, 4 * q)
    out = np.concatenate([np.where(one >= 0, one + h * GLA_DK, -1) for h in range(GLA_HEADS)])
    return out


def _take_cols(w, cols):
    pieces = []
    i = 0
    n = len(cols)
    while i < n:
        j = i + 1
        if cols[i] < 0:
            while j < n and cols[j] < 0:
                j += 1
            pieces.append(jnp.zeros(w.shape[:-1] + (j - i,), w.dtype))
        else:
            while j < n and cols[j] == cols[j - 1] + 1:
                j += 1
            pieces.append(w[..., int(cols[i]):int(cols[i]) + (j - i)])
        i = j
    return jnp.concatenate(pieces, axis=-1)


def _in_proj_offsets():
    o_q = 3 * CONV_W
    o_k = o_q + GLA_HEADS * GLA_DK
    o_v = o_k + GLA_HEADS * GLA_DK
    o_lr = o_v + 2 * GLA_HEADS * GLA_DV
    o_nq = o_lr + 2 * GLA_RANK
    return o_q, o_k, o_v, o_lr, o_nq


def _split_in_proj(w_in):
    o_q, o_k, o_v, o_lr, o_nq = _in_proj_offsets()
    order = _gla_dk_order()
    q_order = order.copy()
    q_order[LR_LANE:LR_LANE + 2 * GLA_RANK] = GLA_HEADS * GLA_DK + np.arange(2 * GLA_RANK)
    q_src = jnp.concatenate([w_in[..., o_q:o_k], w_in[..., o_lr:o_nq]], axis=-1)
    wq = jnp.concatenate([_take_cols(q_src, q_order), _take_cols(w_in[..., o_k:o_v], order)], axis=-1)
    return wq.astype(BF16), w_in[..., o_nq:].astype(BF16)


def _rope_tables(ctx_len, seq):
    q = GLA_DK // 4
    pos = jnp.arange(seq)
    freq = ROPE_THETA ** (-jnp.arange(q, dtype=F32) / q)
    ang_r = (pos // GRID_W).astype(F32)[:, None] * freq[None, :]
    ang_c = (pos % GRID_W).astype(F32)[:, None] * freq[None, :]
    pad = GLA_DKP // 2 - 2 * q
    cos = jnp.concatenate([jnp.cos(ang_r), jnp.cos(ang_c), jnp.ones((seq, pad), F32)], axis=1)
    sin = jnp.concatenate([jnp.sin(ang_r), jnp.sin(ang_c), jnp.zeros((seq, pad), F32)], axis=1)
    cos = jnp.concatenate([jnp.ones((ctx_len, GLA_DKP // 2), F32), cos], axis=0)
    sin = jnp.concatenate([jnp.zeros((ctx_len, GLA_DKP // 2), F32), sin], axis=0)
    return cos, sin


def _na_bias_tables(rpb, rows):
    ngroups = rows // NA_GROUP
    depth, nh, nro, _ = rpb.shape
    q_c = np.arange(GRID_W)
    k_c = np.arange(GRID_W)
    win_c = np.clip(q_c - NA_KC // 2, 0, GRID_W - NA_KC)
    col_ok = (k_c[None, :] >= win_c[:, None]) & (k_c[None, :] < win_c[:, None] + NA_KC)
    col_off = np.clip(k_c[None, :] - q_c[:, None] + NA_KC - 1, 0, 2 * NA_KC - 2)
    blocks = jnp.where(jnp.asarray(col_ok), rpb[:, :, :, col_off].astype(F32), -jnp.inf)
    blocks = jnp.concatenate([blocks, jnp.full((depth, nh, 1, GRID_W, GRID_W), -jnp.inf, F32)], axis=2)
    tabs = []
    q_r = np.arange(NA_GROUP)
    k_r = np.arange(NA_WIN)
    for g in (0, 1, ngroups - 1):
        base = min(max(g * NA_GROUP - NA_KR // 2, 0), rows - NA_WIN)
        r_abs = g * NA_GROUP + q_r
        win_r = np.clip(r_abs - NA_KR // 2, 0, rows - NA_KR)
        key_r = base + k_r
        row_ok = (key_r[None, :] >= win_r[:, None]) & (key_r[None, :] < win_r[:, None] + NA_KR)
        row_off = np.where(row_ok, key_r[None, :] - r_abs[:, None] + NA_KR - 1, nro)
        t = blocks[:, :, row_off]
        tabs.append(t.transpose(0, 1, 2, 4, 3, 5).reshape(depth, nh, NA_GROUP * GRID_W, NA_WIN * GRID_W))
    return jnp.stack(tabs, axis=2)


def kernel(x, c, ctx, c_ctx, mod_a, mod_b, mod_bias, norm_g, ffn_up, ffn_down, w_in, conv_w,
           gla_decay_w, gla_decay_b, gla_norm_g, na_rpb, w_branch_conv, w_branch_gla, w_branch_na,
           w_out, final_g):
    bsz, seq, d = x.shape
    ctx_len = ctx.shape[1]
    depth = mod_a.shape[0]
    slab = ctx_len + seq
    rows = seq // GRID_W
    assert slab % TM == 0 and d % TN == 0 and seq % ctx_len == 0
    assert rows % NA_GROUP == 0 and rows >= NA_WIN and ctx_len % GLA_CHUNK == 0

    mrows = 16
    cvec = jnp.concatenate([c_ctx[None, :], c, jnp.zeros((mrows - 1 - bsz, d), F32)], axis=0)
    mods = _modulation(cvec, mod_a, mod_b, mod_bias)[:, :1 + bsz].reshape(depth, 1 + bsz, 1, N_MOD * d)

    wq, w2 = _split_in_proj(w_in)
    o_q, _, o_v, o_lr, _ = _in_proj_offsets()
    p1_runs = [(0, o_q), (o_v, o_lr - o_v)]
    ffn_up_b = ffn_up.astype(BF16)
    ffn_down_b = ffn_down.astype(BF16).reshape(depth * 2, ffn_down.shape[2], d)
    wb_conv = w_branch_conv.astype(BF16)
    wb_gla = w_branch_gla.astype(BF16)
    wb_na = w_branch_na.astype(BF16)
    w_out_b = w_out.astype(BF16)
    order = _gla_dk_order()
    dw = _take_cols(gla_decay_w, order).reshape(depth, 2, GLA_RANK, GLA_HEADS, GLA_DKP)
    dw = dw.transpose(0, 1, 3, 2, 4)
    wdf = jnp.zeros((depth, GLA_HEADS, 128, GLA_DKP), F32).at[:, :, LR_LANE:LR_LANE + GLA_RANK].set(dw[:, 0])
    wdb = jnp.zeros((depth, GLA_HEADS, 128, GLA_DKP), F32).at[:, :, LR_LANE + GLA_RANK:LR_LANE + 2 * GLA_RANK].set(dw[:, 1])
    wdf = wdf.astype(BF16)
    wdb = wdb.astype(BF16)
    db = _take_cols(gla_decay_b, order).reshape(depth, 2, GLA_HEADS, 1, GLA_DKP)
    ng = gla_norm_g.reshape(depth, GLA_HEADS, 1, GLA_DV)
    cos, sin = _rope_tables(ctx_len, seq)
    na_bias = _na_bias_tables(na_rpb, rows)

    h = jnp.concatenate([ctx, x], axis=1).reshape(bsz * slab, d)
    for l in range(depth):
        m = mods[l]
        nm = functools.partial(_norm_mod, m=m, ctx_len=ctx_len, slab=slab)
        rp = functools.partial(_resid_proj, m=m, ctx_len=ctx_len, slab=slab)
        act = _ffn_up(nm(h, norm_g[l, 0:1], sub=0), ffn_up_b, l, 0)
        h = rp(act, ffn_down_b, 2 * l, h, sub=0, scale=0.5)
        u = nm(h, norm_g[l, 1:2], sub=1)
        p1 = _proj_runs(u, w_in, l, p1_runs)
        pq = _proj(u, wq, l)
        p2 = _proj(u, w2, l)
        ya = _conv_branch(p1, conv_w[l], bsz, ctx_len, slab)
        yb = _gla_branch(p1, pq, cos, sin, wdf[l], wdb[l], db[l, 0], db[l, 1], ng[l], bsz, ctx_len, slab)
        yc = _na_branch(p2, na_bias, l, bsz, ctx_len, slab)
        z = _branch_proj(ya, yb, yc, wb_conv, wb_gla, wb_na, l, p2)
        h = rp(z, w_out_b, l, h, sub=1, scale=1.0)
        act = _ffn_up(nm(h, norm_g[l, 2:3], sub=2), ffn_up_b, l, 1)
        h = rp(act, ffn_down_b, 2 * l + 1, h, sub=2, scale=0.5)
    return _final_norm(h, final_g.reshape(1, d), bsz, ctx_len, seq)
```

```python
import functools

import numpy as np
import jax
import jax.numpy as jnp
from jax import lax
from jax.experimental import pallas as pl
from jax.experimental.pallas import tpu as pltpu

F32 = jnp.float32
BF16 = jnp.bfloat16

GRID_W = 64
N_MOD = 9
CONV_W = 1024
GLA_HEADS = 4
GLA_DK = 192
GLA_DV = 384
GLA_DKP = 256
GLA_RANK = 16
GLA_TAU = 16.0
GLA_CHUNK = 64
NA_HEADS = 12
NA_HD = 128
NA_W = NA_HEADS * NA_HD
NA_KR = 8
NA_KC = 16
NA_GROUP = 4
NA_WIN = NA_KR + NA_GROUP - 1
ROPE_THETA = 10000.0
EPS = 1e-6

P_H, P_BG, P_CG = 0, 1024, 2048
P_V, P_G = 3072, 4608
P_NQ, P_NK, P_NV = 6144, 7680, 9216
P_GA = 10752
PQ_Q, PQ_K = 0, 1024
LR_LANE = 96

VMEM_LIMIT = 56 * 1024 * 1024
TM = 1152
TN = 512


def _cparams(n_axes):
    return pltpu.CompilerParams(dimension_semantics=("arbitrary",) * n_axes,
                                vmem_limit_bytes=VMEM_LIMIT)


def _dot(a, b):
    return jnp.dot(a, b, preferred_element_type=F32)


def _dot_nt(a, b):
    return lax.dot_general(a, b, (((1,), (1,)), ((), ())), preferred_element_type=F32)


def _dot_tn(a, b):
    return lax.dot_general(a, b, (((0,), (0,)), ((), ())), preferred_element_type=F32)


def _mod_a_kernel(c_ref, a_ref, o_ref):
    c = c_ref[...]
    s = (c * jax.nn.sigmoid(c)).astype(BF16)
    o_ref[...] = _dot(s, a_ref[...].astype(BF16))


def _mod_b_kernel(t_ref, b_ref, bias_ref, o_ref):
    o_ref[...] = _dot(t_ref[...].astype(BF16), b_ref[...].astype(BF16)) + bias_ref[...]


def _modulation(cvec, mod_a, mod_b, mod_bias):
    depth, d, rank = mod_a.shape
    rows = cvec.shape[0]
    nm = mod_b.shape[-1]
    tn = min(d, 2048)
    assert nm % tn == 0
    t = pl.pallas_call(
        _mod_a_kernel,
        grid=(depth,),
        in_specs=[pl.BlockSpec((rows, d), lambda l: (0, 0)),
                  pl.BlockSpec((None, d, rank), lambda l: (l, 0, 0))],
        out_specs=pl.BlockSpec((None, rows, rank), lambda l: (l, 0, 0)),
        out_shape=jax.ShapeDtypeStruct((depth, rows, rank), F32),
        compiler_params=_cparams(1),
        name="mod_a",
    )(cvec, mod_a)
    return pl.pallas_call(
        _mod_b_kernel,
        grid=(depth, nm // tn),
        in_specs=[pl.BlockSpec((None, rows, rank), lambda l, j: (l, 0, 0)),
                  pl.BlockSpec((None, rank, tn), lambda l, j: (l, 0, j)),
                  pl.BlockSpec((None, 1, tn), lambda l, j: (l, 0, j))],
        out_specs=pl.BlockSpec((None, rows, tn), lambda l, j: (l, 0, j)),
        out_shape=jax.ShapeDtypeStruct((depth, rows, nm), F32),
        compiler_params=_cparams(2),
        name="mod_b",
    )(t, mod_b, mod_bias.reshape(depth, 1, nm))


def _norm_mod_kernel(h_ref, g_ref, sh_ref, sc_ref, o_ref):
    x = h_ref[...]
    y = x * lax.rsqrt(jnp.mean(x * x, axis=-1, keepdims=True) + EPS) * g_ref[...]
    o_ref[...] = (y * (1.0 + sc_ref[...]) + sh_ref[...]).astype(o_ref.dtype)


def _norm_mod(h, g, m, sub, ctx_len, slab):
    mrows, d = h.shape
    tr = ctx_len
    per = slab // tr

    def mrow(j):
        return jnp.where(j % per == 0, 0, 1 + j // per)

    return pl.pallas_call(
        _norm_mod_kernel,
        grid=(mrows // tr,),
        in_specs=[pl.BlockSpec((tr, d), lambda j: (j, 0)),
                  pl.BlockSpec((1, d), lambda j: (0, 0)),
                  pl.BlockSpec((None, 1, d), lambda j: (mrow(j), 0, 3 * sub)),
                  pl.BlockSpec((None, 1, d), lambda j: (mrow(j), 0, 3 * sub + 1))],
        out_specs=pl.BlockSpec((tr, d), lambda j: (j, 0)),
        out_shape=jax.ShapeDtypeStruct((mrows, d), BF16),
        compiler_params=_cparams(1),
        name="norm_mod",
    )(h, g, m, m)


def _final_norm_kernel(h_ref, g_ref, o_ref):
    x = h_ref[...]
    o_ref[...] = x * lax.rsqrt(jnp.mean(x * x, axis=-1, keepdims=True) + EPS) * g_ref[...]


def _final_norm(h, g, bsz, ctx_len, seq):
    d = h.shape[-1]
    slab = ctx_len + seq
    tr = ctx_len
    return pl.pallas_call(
        _final_norm_kernel,
        grid=(bsz, seq // tr),
        in_specs=[pl.BlockSpec((None, tr, d), lambda b, j: (b, 1 + j, 0)),
                  pl.BlockSpec((1, d), lambda b, j: (0, 0))],
        out_specs=pl.BlockSpec((None, tr, d), lambda b, j: (b, j, 0)),
        out_shape=jax.ShapeDtypeStruct((bsz, seq, d), F32),
        compiler_params=_cparams(2),
        name="final_norm",
    )(h.reshape(bsz, slab, d), g)


def _proj_t_kernel(a_ref, wt_ref, o_ref):
    o_ref[...] = _dot_nt(a_ref[...], wt_ref[0].astype(BF16)).astype(o_ref.dtype)


def _proj_runs(a, wt, l, runs):
    mrows, k = a.shape
    assert all(s % 8 == 0 and n % TN == 0 for s, n in runs)
    starts = np.concatenate([s + TN * np.arange(n // TN) for s, n in runs])
    first = int(starts[0])
    step_at = [int(i) for i in np.nonzero(np.diff(starts) != TN)[0] + 1]
    jumps = [int(starts[i] - starts[i - 1] - TN) for i in step_at]

    def wrow(j):
        r = j * TN + first
        for at, jump in zip(step_at, jumps):
            r = r + jnp.where(j >= at, jump, 0)
        return pl.multiple_of(r, 8)

    return pl.pallas_call(
        _proj_t_kernel,
        grid=(mrows // TM, len(starts)),
        in_specs=[pl.BlockSpec((TM, k), lambda i, j: (i, 0)),
                  pl.BlockSpec((pl.Element(1), pl.Element(TN), pl.Element(k)), lambda i, j: (l, wrow(j), 0))],
        out_specs=pl.BlockSpec((TM, TN), lambda i, j: (i, j)),
        out_shape=jax.ShapeDtypeStruct((mrows, len(starts) * TN), BF16),
        compiler_params=_cparams(2),
        name="in_proj_runs",
    )(a, wt)


def _swiglu_kernel(u_ref, wa_ref, wb_ref, o_ref):
    u = u_ref[...]
    a = _dot(u, wa_ref[...])
    b = _dot(u, wb_ref[...])
    o_ref[...] = (a * jax.nn.sigmoid(a) * b).astype(o_ref.dtype)


def _ffn_up(u, w_up, l, s):
    mrows, k = u.shape
    dff = w_up.shape[-1] // 2
    nb = dff // TN
    return pl.pallas_call(
        _swiglu_kernel,
        grid=(mrows // TM, nb),
        in_specs=[pl.BlockSpec((TM, k), lambda i, j: (i, 0)),
                  pl.BlockSpec((None, None, k, TN), lambda i, j: (l, s, 0, j)),
                  pl.BlockSpec((None, None, k, TN), lambda i, j: (l, s, 0, nb + j))],
        out_specs=pl.BlockSpec((TM, TN), lambda i, j: (i, j)),
        out_shape=jax.ShapeDtypeStruct((mrows, dff), BF16),
        compiler_params=_cparams(2),
        name="ffn_up",
    )(u, w_up, w_up)


def _resid_kernel(a_ref, w_ref, h_ref, gc_ref, gx_ref, o_ref, *, scale, ctx_len, slab):
    tm = a_ref.shape[0]
    acc = _dot(a_ref[...], w_ref[...])
    row0 = (pl.program_id(0) * tm) % slab
    rows = lax.broadcasted_iota(jnp.int32, (tm, 1), 0) + row0
    gate = jnp.where(rows < ctx_len, gc_ref[...], gx_ref[...])
    o_ref[...] = h_ref[...] + (scale * gate) * acc


def _resid_proj(a, w, widx, h, m, sub, scale, ctx_len, slab):
    mrows, k = a.shape
    n = w.shape[-1]
    g0 = (3 * sub + 2) * (n // TN)
    return pl.pallas_call(
        functools.partial(_resid_kernel, scale=scale, ctx_len=ctx_len, slab=slab),
        grid=(mrows // TM, n // TN),
        in_specs=[pl.BlockSpec((TM, k), lambda i, j: (i, 0)),
                  pl.BlockSpec((None, k, TN), lambda i, j: (widx, 0, j)),
                  pl.BlockSpec((TM, TN), lambda i, j: (i, j)),
                  pl.BlockSpec((None, 1, TN), lambda i, j: (0, 0, g0 + j)),
                  pl.BlockSpec((None, 1, TN), lambda i, j: (1 + (i * TM) // slab, 0, g0 + j))],
        out_specs=pl.BlockSpec((TM, TN), lambda i, j: (i, j)),
        out_shape=jax.ShapeDtypeStruct((mrows, n), F32),
        input_output_aliases={2: 0},
        compiler_params=_cparams(2),
        name="resid_proj",
    )(a, w, h, m, m)


def _branch_kernel(a_ref, b_ref, c_ref, pa_ref, pb_ref, pc_ref, ga_ref, gb_ref, gc_ref, o_ref):
    ya = _dot(a_ref[...], pa_ref[...])
    yb = _dot(b_ref[...], pb_ref[...])
    yc = _dot(c_ref[...], pc_ref[...])
    z = (jax.nn.sigmoid(ga_ref[...].astype(F32)) * ya
         + jax.nn.sigmoid(gb_ref[...].astype(F32)) * yb
         + jax.nn.sigmoid(gc_ref[...].astype(F32)) * yc)
    o_ref[...] = z.astype(o_ref.dtype)


def _branch_proj(a, b, c, pa, pb, pc, l, p):
    mrows = a.shape[0]
    d = pa.shape[-1]
    tn = TN
    g0 = P_GA // tn
    nb = d // tn
    return pl.pallas_call(
        _branch_kernel,
        grid=(mrows // TM, nb),
        in_specs=[pl.BlockSpec((TM, a.shape[1]), lambda i, j: (i, 0)),
                  pl.BlockSpec((TM, b.shape[1]), lambda i, j: (i, 0)),
                  pl.BlockSpec((TM, c.shape[1]), lambda i, j: (i, 0)),
                  pl.BlockSpec((None, pa.shape[1], tn), lambda i, j: (l, 0, j)),
                  pl.BlockSpec((None, pb.shape[1], tn), lambda i, j: (l, 0, j)),
                  pl.BlockSpec((None, pc.shape[1], tn), lambda i, j: (l, 0, j)),
                  pl.BlockSpec((TM, tn), lambda i, j: (i, g0 + j)),
                  pl.BlockSpec((TM, tn), lambda i, j: (i, g0 + nb + j)),
                  pl.BlockSpec((TM, tn), lambda i, j: (i, g0 + 2 * nb + j))],
        out_specs=pl.BlockSpec((TM, tn), lambda i, j: (i, j)),
        out_shape=jax.ShapeDtypeStruct((mrows, d), BF16),
        compiler_params=_cparams(2),
        name="branch_proj",
    )(a, b, c, pa, pb, pc, p, p, p)


def _conv_kernel(h_ref, bg_ref, cg_ref, w_ref, o_ref, *, ctx_len):
    n = h_ref.shape[0]
    x = cg_ref[...].astype(F32) * h_ref[...].astype(F32)
    row = lax.broadcasted_iota(jnp.int32, (n, 1), 0)
    first = (row == 0) | (row == ctx_len)
    last = (row == ctx_len - 1) | (row == n - 1)
    prev = jnp.where(first, 0.0, pltpu.roll(x, 1, axis=0))
    nxt = jnp.where(last, 0.0, pltpu.roll(x, n - 1, axis=0))
    w = w_ref[...]
    y = prev * w[0:1, :] + x * w[1:2, :] + nxt * w[2:3, :]
    o_ref[...] = (bg_ref[...].astype(F32) * y).astype(o_ref.dtype)


def _conv_branch(p, conv_w, bsz, ctx_len, slab):
    tc = 256
    nb = CONV_W // tc
    return pl.pallas_call(
        functools.partial(_conv_kernel, ctx_len=ctx_len),
        grid=(bsz, nb),
        in_specs=[pl.BlockSpec((slab, tc), lambda b, j: (b, P_H // tc + j)),
                  pl.BlockSpec((slab, tc), lambda b, j: (b, P_BG // tc + j)),
                  pl.BlockSpec((slab, tc), lambda b, j: (b, P_CG // tc + j)),
                  pl.BlockSpec((conv_w.shape[0], tc), lambda b, j: (0, j))],
        out_specs=pl.BlockSpec((slab, tc), lambda b, j: (b, j)),
        out_shape=jax.ShapeDtypeStruct((bsz * slab, CONV_W), BF16),
        compiler_params=_cparams(2),
        name="conv_branch",
    )(p, p, p, conv_w)


def _log_sigmoid(x):
    return jnp.minimum(x, 0.0) - jnp.log(1.0 + jnp.exp(-jnp.abs(x)))


def _gla_kernel(q_ref, k_ref, v_ref, g_ref, lr_ref, cos_ref, sin_ref, wdf_ref, wdb_ref,
                bdf_ref, bdb_ref, ng_ref, o_ref, qd_s, ki_s, ke_s, dec_s, oacc_s, st_s,
                *, ctx_len):
    n = q_ref.shape[0]
    nchunk = n // GLA_CHUNK
    cchunk = ctx_len // GLA_CHUNK
    half = GLA_DKP // 2
    cos = cos_ref[...]
    sin = sin_ref[...]

    def rope(x):
        x1 = x[:, :half]
        x2 = x[:, half:]
        return jnp.concatenate([x1 * cos - x2 * sin, x1 * sin + x2 * cos], axis=1)

    qr = rope(q_ref[...].astype(F32)) * (GLA_DK ** -0.5)
    kr = rope(k_ref[...].astype(F32))
    lr = lr_ref[...]
    ridx = lax.broadcasted_iota(jnp.int32, (n, 1), 0) % GLA_CHUNK
    ci = lax.broadcasted_iota(jnp.int32, (GLA_CHUNK, GLA_CHUNK), 0)
    cj = lax.broadcasted_iota(jnp.int32, (GLA_CHUNK, GLA_CHUNK), 1)

    def chunk_sums(la):
        cum = la
        for sft in (1, 2, 4, 8, 16, 32):
            cum = cum + jnp.where(ridx >= sft, pltpu.roll(cum, sft, axis=0), 0.0)
        c3 = cum.reshape(nchunk, GLA_CHUNK, GLA_DKP)
        tot3 = c3[:, GLA_CHUNK - 1:GLA_CHUNK, :]
        tot = jnp.broadcast_to(tot3, c3.shape).reshape(n, GLA_DKP)
        return cum, tot, tot3

    def stage(cum, tot, tot3):
        qd_s[...] = (qr * jnp.exp(cum)).astype(BF16)
        ki_s[...] = (kr * jnp.exp(-cum)).astype(BF16)
        ke_s[...] = (kr * jnp.exp(tot - cum)).astype(BF16)
        dec_s[...] = jnp.exp(tot3)

    def scan(chunk_of, steps, mask, first):
        def body(i, carry):
            c = chunk_of(i)
            r0 = pl.multiple_of(c * GLA_CHUNK, GLA_CHUNK)
            qd = qd_s[pl.ds(r0, GLA_CHUNK), :]
            ki = ki_s[pl.ds(r0, GLA_CHUNK), :]
            ke = ke_s[pl.ds(r0, GLA_CHUNK), :]
            vv = v_ref[pl.ds(r0, GLA_CHUNK), :]
            att = jnp.where(mask, _dot_nt(qd, ki), 0.0)
            st = st_s[...]
            o = _dot(att.astype(BF16), vv) + _dot_nt(qd, st.astype(BF16))
            if first:
                oacc_s[pl.ds(r0, GLA_CHUNK), :] = o
            else:
                oacc_s[pl.ds(r0, GLA_CHUNK), :] += o
            st_s[...] = st * dec_s[c] + _dot_tn(vv, ke)
            return carry
        lax.fori_loop(0, steps, body, 0, unroll=4)

    la = _log_sigmoid(_dot(lr, wdf_ref[...]) + bdf_ref[...]) / GLA_TAU
    cum, tot, tot3 = chunk_sums(la)
    stage(cum, tot, tot3)
    st_s[...] = jnp.zeros_like(st_s)
    scan(lambda i: i, nchunk, ci >= cj, True)

    la = _log_sigmoid(_dot(lr, wdb_ref[...]) + bdb_ref[...]) / GLA_TAU
    cum, tot, tot3 = chunk_sums(la)
    stage(tot - cum + la, tot, tot3)
    st_s[...] = jnp.zeros_like(st_s)
    scan(lambda i: cchunk - 1 - i, cchunk, ci <= cj, False)
    scan(lambda i: nchunk - 1 - i, nchunk - cchunk, ci <= cj, False)

    o = oacc_s[...]
    o = o * lax.rsqrt(jnp.mean(o * o, axis=-1, keepdims=True) + EPS) * ng_ref[...]
    g = g_ref[...].astype(F32)
    o_ref[...] = (o * (g * jax.nn.sigmoid(g))).astype(o_ref.dtype)


def _gla_branch(p1, pq, cos, sin, wdf, wdb, bdf, bdb, ng, bsz, ctx_len, slab):
    nchunk = slab // GLA_CHUNK
    head = lambda b, h: (h, 0, 0)
    return pl.pallas_call(
        functools.partial(_gla_kernel, ctx_len=ctx_len),
        grid=(bsz, GLA_HEADS),
        in_specs=[pl.BlockSpec((slab, GLA_DKP), lambda b, h: (b, PQ_Q // GLA_DKP + h)),
                  pl.BlockSpec((slab, GLA_DKP), lambda b, h: (b, PQ_K // GLA_DKP + h)),
                  pl.BlockSpec((slab, GLA_DV), lambda b, h: (b, P_V // GLA_DV + h)),
                  pl.BlockSpec((slab, GLA_DV), lambda b, h: (b, P_G // GLA_DV + h)),
                  pl.BlockSpec((slab, 128), lambda b, h: (b, PQ_Q // 128)),
                  pl.BlockSpec((slab, GLA_DKP // 2), lambda b, h: (0, 0)),
                  pl.BlockSpec((slab, GLA_DKP // 2), lambda b, h: (0, 0)),
                  pl.BlockSpec((None, 128, GLA_DKP), head),
                  pl.BlockSpec((None, 128, GLA_DKP), head),
                  pl.BlockSpec((None, 1, GLA_DKP), head),
                  pl.BlockSpec((None, 1, GLA_DKP), head),
                  pl.BlockSpec((None, 1, GLA_DV), head)],
        out_specs=pl.BlockSpec((slab, GLA_DV), lambda b, h: (b, h)),
        out_shape=jax.ShapeDtypeStruct((bsz * slab, GLA_HEADS * GLA_DV), BF16),
        scratch_shapes=[pltpu.VMEM((slab, GLA_DKP), BF16),
                        pltpu.VMEM((slab, GLA_DKP), BF16),
                        pltpu.VMEM((slab, GLA_DKP), BF16),
                        pltpu.VMEM((nchunk, 1, GLA_DKP), F32),
                        pltpu.VMEM((slab, GLA_DV), F32),
                        pltpu.VMEM((GLA_DV, GLA_DKP), F32)],
        compiler_params=_cparams(2),
        name="gla_branch",
    )(pq, pq, p1, p1, pq, cos, sin, wdf, wdb, bdf, bdb, ng)


def _na_kernel(q_ref, k_ref, v_ref, bias_ref, o_ref, *, ctx_len, rows):
    scale = NA_HD ** -0.5
    gq = NA_GROUP * GRID_W
    nwin = NA_WIN * GRID_W
    kc = k_ref[0:ctx_len, :]
    vc = v_ref[0:ctx_len, :]

    s = _dot_nt(q_ref[0:ctx_len, :], kc) * scale
    e = jnp.exp(s - jnp.max(s, axis=-1, keepdims=True))
    pr = e / jnp.sum(e, axis=-1, keepdims=True)
    o_ref[0:ctx_len, :] = _dot(pr.astype(BF16), vc).astype(o_ref.dtype)

    ngroups = rows // NA_GROUP
    for g in range(ngroups):
        kind = 0 if g == 0 else (2 if g == ngroups - 1 else 1)
        q0 = ctx_len + g * gq
        k0 = ctx_len + min(max(g * NA_GROUP - NA_KR // 2, 0), rows - NA_WIN) * GRID_W
        qg = q_ref[q0:q0 + gq, :]
        s_loc = _dot_nt(qg, k_ref[k0:k0 + nwin, :]) * scale + bias_ref[kind]
        s_ctx = _dot_nt(qg, kc) * scale
        m = jnp.maximum(jnp.max(s_loc, axis=-1, keepdims=True), jnp.max(s_ctx, axis=-1, keepdims=True))
        e_loc = jnp.exp(s_loc - m)
        e_ctx = jnp.exp(s_ctx - m)
        den = jnp.sum(e_loc, axis=-1, keepdims=True) + jnp.sum(e_ctx, axis=-1, keepdims=True)
        o = _dot(e_loc.astype(BF16), v_ref[k0:k0 + nwin, :]) + _dot(e_ctx.astype(BF16), vc)
        o_ref[q0:q0 + gq, :] = (o / den).astype(o_ref.dtype)


def _na_branch(p, bias, l, bsz, ctx_len, slab):
    rows = (slab - ctx_len) // GRID_W
    return pl.pallas_call(
        functools.partial(_na_kernel, ctx_len=ctx_len, rows=rows),
        grid=(NA_HEADS, bsz),
        in_specs=[pl.BlockSpec((slab, NA_HD), lambda h, b: (b, P_NQ // NA_HD + h)),
                  pl.BlockSpec((slab, NA_HD), lambda h, b: (b, P_NK // NA_HD + h)),
                  pl.BlockSpec((slab, NA_HD), lambda h, b: (b, P_NV // NA_HD + h)),
                  pl.BlockSpec((None, None) + bias.shape[2:], lambda h, b: (l, h, 0, 0, 0))],
        out_specs=pl.BlockSpec((slab, NA_HD), lambda h, b: (b, h)),
        out_shape=jax.ShapeDtypeStruct((bsz * slab, NA_W), BF16),
        compiler_params=_cparams(2),
        name="na_branch",
    )(p, p, p, bias)


def _gla_dk_order():
    q = GLA_DK // 4
    one = np.full((GLA_DKP,), -1, np.int64)
    one[0:q] = np.arange(0, q)
    one[q:2 * q] = np.arange(2 * q, 3 * q)
    one[GLA_DKP // 2:GLA_DKP // 2 + q] = np.arange(q, 2 * q)
    one[GLA_DKP // 2 + q:GLA_DKP // 2 + 2 * q] = np.arange(3 * q, 4 * q)
    out = np.concatenate([np.where(one >= 0, one + h * GLA_DK, -1) for h in range(GLA_HEADS)])
    return out


def _take_cols(w, cols, axis=-1):
    axis = axis % w.ndim
    pieces = []
    i = 0
    n = len(cols)
    while i < n:
        j = i + 1
        if cols[i] < 0:
            while j < n and cols[j] < 0:
                j += 1
            pieces.append(jnp.zeros(w.shape[:axis] + (j - i,) + w.shape[axis + 1:], w.dtype))
        else:
            while j < n and cols[j] == cols[j - 1] + 1:
                j += 1
            pieces.append(lax.slice_in_dim(w, int(cols[i]), int(cols[i]) + (j - i), axis=axis))
        i = j
    return jnp.concatenate(pieces, axis=axis)


def _in_proj_offsets():
    o_q = 3 * CONV_W
    o_k = o_q + GLA_HEADS * GLA_DK
    o_v = o_k + GLA_HEADS * GLA_DK
    o_lr = o_v + 2 * GLA_HEADS * GLA_DV
    o_nq = o_lr + 2 * GLA_RANK
    return o_q, o_k, o_v, o_lr, o_nq


def _qk_weights(w_in_t):
    o_q, o_k, o_v, o_lr, o_nq = _in_proj_offsets()
    order = _gla_dk_order()
    q_order = np.where(order >= 0, order + o_q, -1)
    q_order[LR_LANE:LR_LANE + 2 * GLA_RANK] = o_lr + np.arange(2 * GLA_RANK)
    k_order = np.where(order >= 0, order + o_k, -1)
    return _take_cols(w_in_t, np.concatenate([q_order, k_order]), axis=1)


def _rope_tables(ctx_len, seq):
    q = GLA_DK // 4
    pos = jnp.arange(seq)
    freq = ROPE_THETA ** (-jnp.arange(q, dtype=F32) / q)
    ang_r = (pos // GRID_W).astype(F32)[:, None] * freq[None, :]
    ang_c = (pos % GRID_W).astype(F32)[:, None] * freq[None, :]
    pad = GLA_DKP // 2 - 2 * q
    cos = jnp.concatenate([jnp.cos(ang_r), jnp.cos(ang_c), jnp.ones((seq, pad), F32)], axis=1)
    sin = jnp.concatenate([jnp.sin(ang_r), jnp.sin(ang_c), jnp.zeros((seq, pad), F32)], axis=1)
    cos = jnp.concatenate([jnp.ones((ctx_len, GLA_DKP // 2), F32), cos], axis=0)
    sin = jnp.concatenate([jnp.zeros((ctx_len, GLA_DKP // 2), F32), sin], axis=0)
    return cos, sin


def _na_bias_tables(rpb, rows):
    ngroups = rows // NA_GROUP
    depth, nh, nro, _ = rpb.shape
    q_c = np.arange(GRID_W)
    k_c = np.arange(GRID_W)
    win_c = np.clip(q_c - NA_KC // 2, 0, GRID_W - NA_KC)
    col_ok = (k_c[None, :] >= win_c[:, None]) & (k_c[None, :] < win_c[:, None] + NA_KC)
    col_off = np.clip(k_c[None, :] - q_c[:, None] + NA_KC - 1, 0, 2 * NA_KC - 2)
    blocks = jnp.where(jnp.asarray(col_ok), rpb[:, :, :, col_off].astype(F32), -jnp.inf)
    blocks = jnp.concatenate([blocks, jnp.full((depth, nh, 1, GRID_W, GRID_W), -jnp.inf, F32)], axis=2)
    tabs = []
    q_r = np.arange(NA_GROUP)
    k_r = np.arange(NA_WIN)
    for g in (0, 1, ngroups - 1):
        base = min(max(g * NA_GROUP - NA_KR // 2, 0), rows - NA_WIN)
        r_abs = g * NA_GROUP + q_r
        win_r = np.clip(r_abs - NA_KR // 2, 0, rows - NA_KR)
        key_r = base + k_r
        row_ok = (key_r[None, :] >= win_r[:, None]) & (key_r[None, :] < win_r[:, None] + NA_KR)
        row_off = np.where(row_ok, key_r[None, :] - r_abs[:, None] + NA_KR - 1, nro)
        t = blocks[:, :, row_off]
        tabs.append(t.transpose(0, 1, 2, 4, 3, 5).reshape(depth, nh, NA_GROUP * GRID_W, NA_WIN * GRID_W))
    return jnp.stack(tabs, axis=2)


def kernel(x, c, ctx, c_ctx, mod_a, mod_b, mod_bias, norm_g, ffn_up, ffn_down, w_in, conv_w,
           gla_decay_w, gla_decay_b, gla_norm_g, na_rpb, w_branch_conv, w_branch_gla, w_branch_na,
           w_out, final_g):
    bsz, seq, d = x.shape
    ctx_len = ctx.shape[1]
    depth = mod_a.shape[0]
    slab = ctx_len + seq
    rows = seq // GRID_W
    assert slab % TM == 0 and d % TN == 0 and seq % ctx_len == 0
    assert rows % NA_GROUP == 0 and rows >= NA_WIN and ctx_len % GLA_CHUNK == 0

    mrows = 16
    cvec = jnp.concatenate([c_ctx[None, :], c, jnp.zeros((mrows - 1 - bsz, d), F32)], axis=0)
    mods = _modulation(cvec, mod_a, mod_b, mod_bias)[:, :1 + bsz].reshape(depth, 1 + bsz, 1, N_MOD * d)

    o_q, _, o_v, o_lr, o_nq = _in_proj_offsets()
    p_runs = [(0, o_q), (o_v, o_lr - o_v), (o_nq, w_in.shape[-1] - o_nq)]
    w_in_t = jnp.swapaxes(w_in, 1, 2)
    wq_t = _qk_weights(w_in_t)
    ffn_up_b = ffn_up.astype(BF16)
    ffn_down_b = ffn_down.astype(BF16).reshape(depth * 2, ffn_down.shape[2], d)
    wb_conv = w_branch_conv.astype(BF16)
    wb_gla = w_branch_gla.astype(BF16)
    wb_na = w_branch_na.astype(BF16)
    w_out_b = w_out.astype(BF16)
    order = _gla_dk_order()
    dw = _take_cols(gla_decay_w, order).reshape(depth, 2, GLA_RANK, GLA_HEADS, GLA_DKP)
    dw = dw.transpose(0, 1, 3, 2, 4)
    wdf = jnp.zeros((depth, GLA_HEADS, 128, GLA_DKP), F32).at[:, :, LR_LANE:LR_LANE + GLA_RANK].set(dw[:, 0])
    wdb = jnp.zeros((depth, GLA_HEADS, 128, GLA_DKP), F32).at[:, :, LR_LANE + GLA_RANK:LR_LANE + 2 * GLA_RANK].set(dw[:, 1])
    wdf = wdf.astype(BF16)
    wdb = wdb.astype(BF16)
    db = _take_cols(gla_decay_b, order).reshape(depth, 2, GLA_HEADS, 1, GLA_DKP)
    ng = gla_norm_g.reshape(depth, GLA_HEADS, 1, GLA_DV)
    cos, sin = _rope_tables(ctx_len, seq)
    na_bias = _na_bias_tables(na_rpb, rows)

    h = jnp.concatenate([ctx, x], axis=1).reshape(bsz * slab, d)
    for l in range(depth):
        m = mods[l]
        nm = functools.partial(_norm_mod, m=m, ctx_len=ctx_len, slab=slab)
        rp = functools.partial(_resid_proj, m=m, ctx_len=ctx_len, slab=slab)
        act = _ffn_up(nm(h, norm_g[l, 0:1], sub=0), ffn_up_b, l, 0)
        h = rp(act, ffn_down_b, 2 * l, h, sub=0, scale=0.5)
        u = nm(h, norm_g[l, 1:2], sub=1)
        p = _proj_runs(u, w_in_t, l, p_runs)
        pq = _proj_runs(u, wq_t, l, [(0, wq_t.shape[1])])
        ya = _conv_branch(p, conv_w[l], bsz, ctx_len, slab)
        yb = _gla_branch(p, pq, cos, sin, wdf[l], wdb[l], db[l, 0], db[l, 1], ng[l], bsz, ctx_len, slab)
        yc = _na_branch(p, na_bias, l, bsz, ctx_len, slab)
        z = _branch_proj(ya, yb, yc, wb_conv, wb_gla, wb_na, l, p)
        h = rp(z, w_out_b, l, h, sub=1, scale=1.0)
        act = _ffn_up(nm(h, norm_g[l, 2:3], sub=2), ffn_up_b, l, 1)
        h = rp(act, ffn_down_b, 2 * l + 1, h, sub=2, scale=0.5)
    return _final_norm(h, final_g.reshape(1, d), bsz, ctx_len, seq)
```

```python
import functools

import numpy as np
import jax
import jax.numpy as jnp
from jax import lax
from jax.experimental import pallas as pl
from jax.experimental.pallas import tpu as pltpu

F32 = jnp.float32
BF16 = jnp.bfloat16

GRID_W = 64
N_MOD = 9
CONV_W = 1024
GLA_HEADS = 4
GLA_DK = 192
GLA_DV = 384
GLA_DKP = 256
GLA_RANK = 16
GLA_TAU = 16.0
GLA_CHUNK = 64
NA_HEADS = 12
NA_HD = 128
NA_W = NA_HEADS * NA_HD
NA_KR = 8
NA_KC = 16
NA_GROUP = 4
NA_WIN = NA_KR + NA_GROUP - 1
ROPE_THETA = 10000.0
EPS = 1e-6

P_H, P_BG, P_CG = 0, 1024, 2048
P_V, P_G = 3072, 4608
P_NQ, P_NK, P_NV = 6144, 7680, 9216
P_GA = 10752
PQ_Q, PQ_K = 0, 1024
LR_LANE = 96

VMEM_LIMIT = 56 * 1024 * 1024
TM = 1152
TN = 512
NORM_ROWS = 768
NORM_SUB_ROWS = 128


def _cparams(n_axes):
    return pltpu.CompilerParams(dimension_semantics=("arbitrary",) * n_axes,
                                vmem_limit_bytes=VMEM_LIMIT)


def _dot(a, b):
    return jnp.dot(a, b, preferred_element_type=F32)


def _dot_nt(a, b):
    return lax.dot_general(a, b, (((1,), (1,)), ((), ())), preferred_element_type=F32)


def _dot_tn(a, b):
    return lax.dot_general(a, b, (((0,), (0,)), ((), ())), preferred_element_type=F32)


def _mod_a_kernel(c_ref, a_ref, o_ref):
    c = c_ref[...]
    s = (c * jax.nn.sigmoid(c)).astype(BF16)
    o_ref[...] = _dot(s, a_ref[...].astype(BF16))


def _mod_b_kernel(t_ref, b_ref, bias_ref, o_ref):
    o_ref[...] = _dot(t_ref[...].astype(BF16), b_ref[...].astype(BF16)) + bias_ref[...]


def _modulation(cvec, mod_a, mod_b, mod_bias):
    depth, d, rank = mod_a.shape
    rows = cvec.shape[0]
    nm = mod_b.shape[-1]
    tn = min(d, 2048)
    assert nm % tn == 0
    t = pl.pallas_call(
        _mod_a_kernel,
        grid=(depth,),
        in_specs=[pl.BlockSpec((rows, d), lambda l: (0, 0)),
                  pl.BlockSpec((None, d, rank), lambda l: (l, 0, 0))],
        out_specs=pl.BlockSpec((None, rows, rank), lambda l: (l, 0, 0)),
        out_shape=jax.ShapeDtypeStruct((depth, rows, rank), F32),
        compiler_params=_cparams(1),
        name="mod_a",
    )(cvec, mod_a)
    return pl.pallas_call(
        _mod_b_kernel,
        grid=(depth, nm // tn),
        in_specs=[pl.BlockSpec((None, rows, rank), lambda l, j: (l, 0, 0)),
                  pl.BlockSpec((None, rank, tn), lambda l, j: (l, 0, j)),
                  pl.BlockSpec((None, 1, tn), lambda l, j: (l, 0, j))],
        out_specs=pl.BlockSpec((None, rows, tn), lambda l, j: (l, 0, j)),
        out_shape=jax.ShapeDtypeStruct((depth, rows, nm), F32),
        compiler_params=_cparams(2),
        name="mod_b",
    )(t, mod_b, mod_bias.reshape(depth, 1, nm))


def _norm_mod_kernel(h_ref, g_ref, shc_ref, scc_ref, shx_ref, scx_ref, o_ref, *, ctx_len, slab):
    tr = h_ref.shape[0]
    sub = NORM_SUB_ROWS
    row0 = (pl.program_id(0) * tr) % slab
    for r in range(0, tr, sub):
        x = h_ref[r:r + sub, :]
        y = x * lax.rsqrt(jnp.mean(x * x, axis=-1, keepdims=True) + EPS) * g_ref[...]
        rows = lax.broadcasted_iota(jnp.int32, (sub, 1), 0) + (row0 + r)
        is_ctx = rows < ctx_len
        sc = jnp.where(is_ctx, scc_ref[...], scx_ref[...])
        sh = jnp.where(is_ctx, shc_ref[...], shx_ref[...])
        o_ref[r:r + sub, :] = (y * (1.0 + sc) + sh).astype(o_ref.dtype)


def _norm_mod(h, g, m, sub, ctx_len, slab):
    mrows, d = h.shape
    tr = NORM_ROWS
    assert slab % tr == 0
    ctx_row = lambda c: (lambda j: (0, 0, c))
    batch_row = lambda c: (lambda j: (1 + (j * tr) // slab, 0, c))
    return pl.pallas_call(
        functools.partial(_norm_mod_kernel, ctx_len=ctx_len, slab=slab),
        grid=(mrows // tr,),
        in_specs=[pl.BlockSpec((tr, d), lambda j: (j, 0)),
                  pl.BlockSpec((1, d), lambda j: (0, 0)),
                  pl.BlockSpec((None, 1, d), ctx_row(3 * sub)),
                  pl.BlockSpec((None, 1, d), ctx_row(3 * sub + 1)),
                  pl.BlockSpec((None, 1, d), batch_row(3 * sub)),
                  pl.BlockSpec((None, 1, d), batch_row(3 * sub + 1))],
        out_specs=pl.BlockSpec((tr, d), lambda j: (j, 0)),
        out_shape=jax.ShapeDtypeStruct((mrows, d), BF16),
        compiler_params=_cparams(1),
        name="norm_mod",
    )(h, g, m, m, m, m)


def _final_norm_kernel(h_ref, g_ref, o_ref):
    x = h_ref[...]
    o_ref[...] = x * lax.rsqrt(jnp.mean(x * x, axis=-1, keepdims=True) + EPS) * g_ref[...]


def _final_norm(h, g, bsz, ctx_len, seq):
    d = h.shape[-1]
    slab = ctx_len + seq
    tr = ctx_len
    return pl.pallas_call(
        _final_norm_kernel,
        grid=(bsz, seq // tr),
        in_specs=[pl.BlockSpec((None, tr, d), lambda b, j: (b, 1 + j, 0)),
                  pl.BlockSpec((1, d), lambda b, j: (0, 0))],
        out_specs=pl.BlockSpec((None, tr, d), lambda b, j: (b, j, 0)),
        out_shape=jax.ShapeDtypeStruct((bsz, seq, d), F32),
        compiler_params=_cparams(2),
        name="final_norm",
    )(h.reshape(bsz, slab, d), g)


def _proj_t_kernel(a_ref, wt_ref, o_ref):
    o_ref[...] = _dot_nt(a_ref[...], wt_ref[0].astype(BF16)).astype(o_ref.dtype)


def _proj_runs(a, wt, l, runs):
    mrows, k = a.shape
    assert all(s % 8 == 0 and n % TN == 0 for s, n in runs)
    starts = np.concatenate([s + TN * np.arange(n // TN) for s, n in runs])
    first = int(starts[0])
    step_at = [int(i) for i in np.nonzero(np.diff(starts) != TN)[0] + 1]
    jumps = [int(starts[i] - starts[i - 1] - TN) for i in step_at]

    def wrow(j):
        r = j * TN + first
        for at, jump in zip(step_at, jumps):
            r = r + jnp.where(j >= at, jump, 0)
        return pl.multiple_of(r, 8)

    tm = 2 * TM
    return pl.pallas_call(
        _proj_t_kernel,
        grid=(mrows // tm, len(starts)),
        in_specs=[pl.BlockSpec((tm, k), lambda i, j: (i, 0), pipeline_mode=pl.Buffered(1)),
                  pl.BlockSpec((pl.Element(1), pl.Element(TN), pl.Element(k)), lambda i, j: (l, wrow(j), 0))],
        out_specs=pl.BlockSpec((tm, TN), lambda i, j: (i, j)),
        out_shape=jax.ShapeDtypeStruct((mrows, len(starts) * TN), BF16),
        compiler_params=_cparams(2),
        name="in_proj_runs",
    )(a, wt)


def _swiglu_kernel(u_ref, wa_ref, wb_ref, o_ref):
    u = u_ref[...]
    a = _dot(u, wa_ref[...])
    b = _dot(u, wb_ref[...])
    o_ref[...] = (a * jax.nn.sigmoid(a) * b).astype(o_ref.dtype)


def _ffn_up(u, w_up, l, s):
    mrows, k = u.shape
    dff = w_up.shape[-1] // 2
    nb = dff // TN
    return pl.pallas_call(
        _swiglu_kernel,
        grid=(mrows // TM, nb),
        in_specs=[pl.BlockSpec((TM, k), lambda i, j: (i, 0)),
                  pl.BlockSpec((None, None, k, TN), lambda i, j: (l, s, 0, j)),
                  pl.BlockSpec((None, None, k, TN), lambda i, j: (l, s, 0, nb + j))],
        out_specs=pl.BlockSpec((TM, TN), lambda i, j: (i, j)),
        out_shape=jax.ShapeDtypeStruct((mrows, dff), BF16),
        compiler_params=_cparams(2),
        name="ffn_up",
    )(u, w_up, w_up)


def _resid_kernel(a_ref, w_ref, h_ref, gc_ref, gx_ref, o_ref, *, scale, ctx_len, slab):
    tm = a_ref.shape[0]
    acc = _dot(a_ref[...], w_ref[...])
    row0 = (pl.program_id(0) * tm) % slab
    rows = lax.broadcasted_iota(jnp.int32, (tm, 1), 0) + row0
    gate = jnp.where(rows < ctx_len, gc_ref[...], gx_ref[...])
    o_ref[...] = h_ref[...] + (scale * gate) * acc


def _resid_proj(a, w, widx, h, m, sub, scale, ctx_len, slab):
    mrows, k = a.shape
    n = w.shape[-1]
    g0 = (3 * sub + 2) * (n // TN)
    return pl.pallas_call(
        functools.partial(_resid_kernel, scale=scale, ctx_len=ctx_len, slab=slab),
        grid=(mrows // TM, n // TN),
        in_specs=[pl.BlockSpec((TM, k), lambda i, j: (i, 0)),
                  pl.BlockSpec((None, k, TN), lambda i, j: (widx, 0, j)),
                  pl.BlockSpec((TM, TN), lambda i, j: (i, j)),
                  pl.BlockSpec((None, 1, TN), lambda i, j: (0, 0, g0 + j)),
                  pl.BlockSpec((None, 1, TN), lambda i, j: (1 + (i * TM) // slab, 0, g0 + j))],
        out_specs=pl.BlockSpec((TM, TN), lambda i, j: (i, j)),
        out_shape=jax.ShapeDtypeStruct((mrows, n), F32),
        input_output_aliases={2: 0},
        compiler_params=_cparams(2),
        name="resid_proj",
    )(a, w, h, m, m)


def _branch_kernel(a_ref, b_ref, c_ref, pa_ref, pb_ref, pc_ref, ga_ref, gb_ref, gc_ref, o_ref):
    ya = _dot(a_ref[...], pa_ref[...])
    yb = _dot(b_ref[...], pb_ref[...])
    yc = _dot(c_ref[...], pc_ref[...])
    z = (jax.nn.sigmoid(ga_ref[...].astype(F32)) * ya
         + jax.nn.sigmoid(gb_ref[...].astype(F32)) * yb
         + jax.nn.sigmoid(gc_ref[...].astype(F32)) * yc)
    o_ref[...] = z.astype(o_ref.dtype)


def _branch_proj(a, b, c, pa, pb, pc, l, p):
    mrows = a.shape[0]
    d = pa.shape[-1]
    tn = TN
    g0 = P_GA // tn
    nb = d // tn
    return pl.pallas_call(
        _branch_kernel,
        grid=(mrows // TM, nb),
        in_specs=[pl.BlockSpec((TM, a.shape[1]), lambda i, j: (i, 0)),
                  pl.BlockSpec((TM, b.shape[1]), lambda i, j: (i, 0)),
                  pl.BlockSpec((TM, c.shape[1]), lambda i, j: (i, 0)),
                  pl.BlockSpec((None, pa.shape[1], tn), lambda i, j: (l, 0, j)),
                  pl.BlockSpec((None, pb.shape[1], tn), lambda i, j: (l, 0, j)),
                  pl.BlockSpec((None, pc.shape[1], tn), lambda i, j: (l, 0, j)),
                  pl.BlockSpec((TM, tn), lambda i, j: (i, g0 + j)),
                  pl.BlockSpec((TM, tn), lambda i, j: (i, g0 + nb + j)),
                  pl.BlockSpec((TM, tn), lambda i, j: (i, g0 + 2 * nb + j))],
        out_specs=pl.BlockSpec((TM, tn), lambda i, j: (i, j)),
        out_shape=jax.ShapeDtypeStruct((mrows, d), BF16),
        compiler_params=_cparams(2),
        name="branch_proj",
    )(a, b, c, pa, pb, pc, p, p, p)


def _conv_kernel(h_ref, bg_ref, cg_ref, w_ref, o_ref, *, ctx_len):
    n = h_ref.shape[0]
    x = cg_ref[...].astype(F32) * h_ref[...].astype(F32)
    row = lax.broadcasted_iota(jnp.int32, (n, 1), 0)
    first = (row == 0) | (row == ctx_len)
    last = (row == ctx_len - 1) | (row == n - 1)
    prev = jnp.where(first, 0.0, pltpu.roll(x, 1, axis=0))
    nxt = jnp.where(last, 0.0, pltpu.roll(x, n - 1, axis=0))
    w = w_ref[...]
    y = prev * w[0:1, :] + x * w[1:2, :] + nxt * w[2:3, :]
    o_ref[...] = (bg_ref[...].astype(F32) * y).astype(o_ref.dtype)


def _conv_branch(p, conv_w, bsz, ctx_len, slab):
    tc = 256
    nb = CONV_W // tc
    return pl.pallas_call(
        functools.partial(_conv_kernel, ctx_len=ctx_len),
        grid=(bsz, nb),
        in_specs=[pl.BlockSpec((slab, tc), lambda b, j: (b, P_H // tc + j)),
                  pl.BlockSpec((slab, tc), lambda b, j: (b, P_BG // tc + j)),
                  pl.BlockSpec((slab, tc), lambda b, j: (b, P_CG // tc + j)),
                  pl.BlockSpec((conv_w.shape[0], tc), lambda b, j: (0, j))],
        out_specs=pl.BlockSpec((slab, tc), lambda b, j: (b, j)),
        out_shape=jax.ShapeDtypeStruct((bsz * slab, CONV_W), BF16),
        compiler_params=_cparams(2),
        name="conv_branch",
    )(p, p, p, conv_w)


def _log_sigmoid(x):
    return jnp.minimum(x, 0.0) - jnp.log(1.0 + jnp.exp(-jnp.abs(x)))


def _gla_kernel(q_ref, k_ref, v_ref, g_ref, lr_ref, cos_ref, sin_ref, wdf_ref, wdb_ref,
                bdf_ref, bdb_ref, ng_ref, o_ref, qd_s, q2_s, ki_s, ke_s, dec_s, oacc_s, st_s,
                *, ctx_len):
    n = q_ref.shape[0]
    pair = 2 * GLA_CHUNK
    npair = n // pair
    cpair = ctx_len // pair
    half = GLA_DKP // 2
    cos = cos_ref[...]
    sin = sin_ref[...]

    def rope(x):
        x1 = x[:, :half]
        x2 = x[:, half:]
        return jnp.concatenate([x1 * cos - x2 * sin, x1 * sin + x2 * cos], axis=1)

    qr = rope(q_ref[...].astype(F32)) * (GLA_DK ** -0.5)
    kr = rope(k_ref[...].astype(F32))
    lr = lr_ref[...]
    ridx = lax.broadcasted_iota(jnp.int32, (n, 1), 0) % GLA_CHUNK
    pi = lax.broadcasted_iota(jnp.int32, (pair, pair), 0)
    pj = lax.broadcasted_iota(jnp.int32, (pair, pair), 1)
    same = (pi < GLA_CHUNK) == (pj < GLA_CHUNK)

    def spread(t4):
        return jnp.broadcast_to(t4, (npair, 2, GLA_CHUNK, GLA_DKP)).reshape(n, GLA_DKP)

    def chunk_sums(la):
        cum = la
        for sft in (1, 2, 4, 8, 16, 32):
            cum = cum + jnp.where(ridx >= sft, pltpu.roll(cum, sft, axis=0), 0.0)
        tot4 = cum.reshape(npair, 2, GLA_CHUNK, GLA_DKP)[:, :, GLA_CHUNK - 1:GLA_CHUNK, :]
        return cum, tot4

    def stage(cum, tot4, forward):
        lo, hi = tot4[:, 0:1], tot4[:, 1:2]
        zero = jnp.zeros_like(lo)
        if forward:
            q_add = spread(jnp.concatenate([zero, lo], axis=1))
            k_add = spread(jnp.concatenate([hi, zero], axis=1))
        else:
            q_add = spread(jnp.concatenate([hi, zero], axis=1))
            k_add = spread(jnp.concatenate([zero, lo], axis=1))
        qd_s[...] = (qr * jnp.exp(cum)).astype(BF16)
        q2_s[...] = (qr * jnp.exp(cum + q_add)).astype(BF16)
        ki_s[...] = (kr * jnp.exp(-cum)).astype(BF16)
        ke_s[...] = (kr * jnp.exp(spread(tot4) - cum + k_add)).astype(BF16)
        dec_s[...] = jnp.exp(lo[:, 0] + hi[:, 0])

    def scan(pair_of, steps, diag, off, first):
        def body(i, carry):
            c = pair_of(i)
            rows = pl.ds(pl.multiple_of(c * pair, pair), pair)
            q2 = q2_s[rows, :]
            ki = ki_s[rows, :]
            vv = v_ref[rows, :]
            att = jnp.where(diag, _dot_nt(qd_s[rows, :], ki), jnp.where(off, _dot_nt(q2, ki), 0.0))
            st = st_s[...]
            o = _dot(att.astype(BF16), vv) + _dot_nt(q2, st.astype(BF16))
            if first:
                oacc_s[rows, :] = o
            else:
                oacc_s[rows, :] += o
            st_s[...] = st * dec_s[c] + _dot_tn(vv, ke_s[rows, :])
            return carry
        lax.fori_loop(0, steps, body, 0, unroll=2)

    la = _log_sigmoid(_dot(lr, wdf_ref[...]) + bdf_ref[...]) / GLA_TAU
    cum, tot4 = chunk_sums(la)
    stage(cum, tot4, True)
    st_s[...] = jnp.zeros_like(st_s)
    scan(lambda i: i, npair, same & (pi >= pj), pi > pj, True)

    la = _log_sigmoid(_dot(lr, wdb_ref[...]) + bdb_ref[...]) / GLA_TAU
    cum, tot4 = chunk_sums(la)
    stage(spread(tot4) - cum + la, tot4, False)
    st_s[...] = jnp.zeros_like(st_s)
    scan(lambda i: cpair - 1 - i, cpair, same & (pi <= pj), pi < pj, False)
    scan(lambda i: npair - 1 - i, npair - cpair, same & (pi <= pj), pi < pj, False)

    o = oacc_s[...]
    o = o * lax.rsqrt(jnp.mean(o * o, axis=-1, keepdims=True) + EPS) * ng_ref[...]
    g = g_ref[...].astype(F32)
    o_ref[...] = (o * (g * jax.nn.sigmoid(g))).astype(o_ref.dtype)


def _gla_branch(p1, pq, cos, sin, wdf, wdb, bdf, bdb, ng, bsz, ctx_len, slab):
    npair = slab // (2 * GLA_CHUNK)
    head = lambda b, h: (h, 0, 0)
    return pl.pallas_call(
        functools.partial(_gla_kernel, ctx_len=ctx_len),
        grid=(bsz, GLA_HEADS),
        in_specs=[pl.BlockSpec((slab, GLA_DKP), lambda b, h: (b, PQ_Q // GLA_DKP + h)),
                  pl.BlockSpec((slab, GLA_DKP), lambda b, h: (b, PQ_K // GLA_DKP + h)),
                  pl.BlockSpec((slab, GLA_DV), lambda b, h: (b, P_V // GLA_DV + h)),
                  pl.BlockSpec((slab, GLA_DV), lambda b, h: (b, P_G // GLA_DV + h)),
                  pl.BlockSpec((slab, 128), lambda b, h: (b, PQ_Q // 128)),
                  pl.BlockSpec((slab, GLA_DKP // 2), lambda b, h: (0, 0)),
                  pl.BlockSpec((slab, GLA_DKP // 2), lambda b, h: (0, 0)),
                  pl.BlockSpec((None, 128, GLA_DKP), head),
                  pl.BlockSpec((None, 128, GLA_DKP), head),
                  pl.BlockSpec((None, 1, GLA_DKP), head),
                  pl.BlockSpec((None, 1, GLA_DKP), head),
                  pl.BlockSpec((None, 1, GLA_DV), head)],
        out_specs=pl.BlockSpec((slab, GLA_DV), lambda b, h: (b, h)),
        out_shape=jax.ShapeDtypeStruct((bsz * slab, GLA_HEADS * GLA_DV), BF16),
        scratch_shapes=[pltpu.VMEM((slab, GLA_DKP), BF16),
                        pltpu.VMEM((slab, GLA_DKP), BF16),
                        pltpu.VMEM((slab, GLA_DKP), BF16),
                        pltpu.VMEM((slab, GLA_DKP), BF16),
                        pltpu.VMEM((npair, 1, GLA_DKP), F32),
                        pltpu.VMEM((slab, GLA_DV), F32),
                        pltpu.VMEM((GLA_DV, GLA_DKP), F32)],
        compiler_params=_cparams(2),
        name="gla_branch",
    )(pq, pq, p1, p1, pq, cos, sin, wdf, wdb, bdf, bdb, ng)


def _na_kernel(q_ref, k_ref, v_ref, bias_ref, o_ref, *, ctx_len, rows):
    scale = NA_HD ** -0.5
    gq = NA_GROUP * GRID_W
    nwin = NA_WIN * GRID_W
    kc = k_ref[0:ctx_len, :]
    vc = v_ref[0:ctx_len, :]

    s = _dot_nt(q_ref[0:ctx_len, :], kc) * scale
    e = jnp.exp(s - jnp.max(s, axis=-1, keepdims=True))
    pr = e / jnp.sum(e, axis=-1, keepdims=True)
    o_ref[0:ctx_len, :] = _dot(pr.astype(BF16), vc).astype(o_ref.dtype)

    ngroups = rows // NA_GROUP
    for g in range(ngroups):
        kind = 0 if g == 0 else (2 if g == ngroups - 1 else 1)
        q0 = ctx_len + g * gq
        k0 = ctx_len + min(max(g * NA_GROUP - NA_KR // 2, 0), rows - NA_WIN) * GRID_W
        qg = q_ref[q0:q0 + gq, :]
        s_loc = _dot_nt(qg, k_ref[k0:k0 + nwin, :]) * scale + bias_ref[kind]
        s_ctx = _dot_nt(qg, kc) * scale
        m = jnp.maximum(jnp.max(s_loc, axis=-1, keepdims=True), jnp.max(s_ctx, axis=-1, keepdims=True))
        e_loc = jnp.exp(s_loc - m)
        e_ctx = jnp.exp(s_ctx - m)
        den = jnp.sum(e_loc, axis=-1, keepdims=True) + jnp.sum(e_ctx, axis=-1, keepdims=True)
        o = _dot(e_loc.astype(BF16), v_ref[k0:k0 + nwin, :]) + _dot(e_ctx.astype(BF16), vc)
        o_ref[q0:q0 + gq, :] = (o / den).astype(o_ref.dtype)


def _na_branch(p, bias, l, bsz, ctx_len, slab):
    rows = (slab - ctx_len) // GRID_W
    return pl.pallas_call(
        functools.partial(_na_kernel, ctx_len=ctx_len, rows=rows),
        grid=(NA_HEADS, bsz),
        in_specs=[pl.BlockSpec((slab, NA_HD), lambda h, b: (b, P_NQ // NA_HD + h)),
                  pl.BlockSpec((slab, NA_HD), lambda h, b: (b, P_NK // NA_HD + h)),
                  pl.BlockSpec((slab, NA_HD), lambda h, b: (b, P_NV // NA_HD + h)),
                  pl.BlockSpec((None, None) + bias.shape[2:], lambda h, b: (l, h, 0, 0, 0))],
        out_specs=pl.BlockSpec((slab, NA_HD), lambda h, b: (b, h)),
        out_shape=jax.ShapeDtypeStruct((bsz * slab, NA_W), BF16),
        compiler_params=_cparams(2),
        name="na_branch",
    )(p, p, p, bias)


def _gla_dk_order():
    q = GLA_DK // 4
    one = np.full((GLA_DKP,), -1, np.int64)
    one[0:q] = np.arange(0, q)
    one[q:2 * q] = np.arange(2 * q, 3 * q)
    one[GLA_DKP // 2:GLA_DKP // 2 + q] = np.arange(q, 2 * q)
    one[GLA_DKP // 2 + q:GLA_DKP // 2 + 2 * q] = np.arange(3 * q, 4 * q)
    out = np.concatenate([np.where(one >= 0, one + h * GLA_DK, -1) for h in range(GLA_HEADS)])
    return out


def _take_cols(w, cols, axis=-1):
    axis = axis % w.ndim
    pieces = []
    i = 0
    n = len(cols)
    while i < n:
        j = i + 1
        if cols[i] < 0:
            while j < n and cols[j] < 0:
                j += 1
            pieces.append(jnp.zeros(w.shape[:axis] + (j - i,) + w.shape[axis + 1:], w.dtype))
        else:
            while j < n and cols[j] == cols[j - 1] + 1:
                j += 1
            pieces.append(lax.slice_in_dim(w, int(cols[i]), int(cols[i]) + (j - i), axis=axis))
        i = j
    return jnp.concatenate(pieces, axis=axis)


def _in_proj_offsets():
    o_q = 3 * CONV_W
    o_k = o_q + GLA_HEADS * GLA_DK
    o_v = o_k + GLA_HEADS * GLA_DK
    o_lr = o_v + 2 * GLA_HEADS * GLA_DV
    o_nq = o_lr + 2 * GLA_RANK
    return o_q, o_k, o_v, o_lr, o_nq


def _qk_weights(w_in_t):
    o_q, o_k, o_v, o_lr, o_nq = _in_proj_offsets()
    order = _gla_dk_order()
    q_order = np.where(order >= 0, order + o_q, -1)
    q_order[LR_LANE:LR_LANE + 2 * GLA_RANK] = o_lr + np.arange(2 * GLA_RANK)
    k_order = np.where(order >= 0, order + o_k, -1)
    return _take_cols(w_in_t, np.concatenate([q_order, k_order]), axis=1)


def _rope_tables(ctx_len, seq):
    q = GLA_DK // 4
    pos = jnp.arange(seq)
    freq = ROPE_THETA ** (-jnp.arange(q, dtype=F32) / q)
    ang_r = (pos // GRID_W).astype(F32)[:, None] * freq[None, :]
    ang_c = (pos % GRID_W).astype(F32)[:, None] * freq[None, :]
    pad = GLA_DKP // 2 - 2 * q
    cos = jnp.concatenate([jnp.cos(ang_r), jnp.cos(ang_c), jnp.ones((seq, pad), F32)], axis=1)
    sin = jnp.concatenate([jnp.sin(ang_r), jnp.sin(ang_c), jnp.zeros((seq, pad), F32)], axis=1)
    cos = jnp.concatenate([jnp.ones((ctx_len, GLA_DKP // 2), F32), cos], axis=0)
    sin = jnp.concatenate([jnp.zeros((ctx_len, GLA_DKP // 2), F32), sin], axis=0)
    return cos, sin


def _na_bias_tables(rpb, rows):
    ngroups = rows // NA_GROUP
    depth, nh, nro, _ = rpb.shape
    q_c = np.arange(GRID_W)
    k_c = np.arange(GRID_W)
    win_c = np.clip(q_c - NA_KC // 2, 0, GRID_W - NA_KC)
    col_ok = (k_c[None, :] >= win_c[:, None]) & (k_c[None, :] < win_c[:, None] + NA_KC)
    col_off = np.clip(k_c[None, :] - q_c[:, None] + NA_KC - 1, 0, 2 * NA_KC - 2)
    blocks = jnp.where(jnp.asarray(col_ok), rpb[:, :, :, col_off].astype(F32), -jnp.inf)
    blocks = jnp.concatenate([blocks, jnp.full((depth, nh, 1, GRID_W, GRID_W), -jnp.inf, F32)], axis=2)
    tabs = []
    q_r = np.arange(NA_GROUP)
    k_r = np.arange(NA_WIN)
    for g in (0, 1, ngroups - 1):
        base = min(max(g * NA_GROUP - NA_KR // 2, 0), rows - NA_WIN)
        r_abs = g * NA_GROUP + q_r
        win_r = np.clip(r_abs - NA_KR // 2, 0, rows - NA_KR)
        key_r = base + k_r
        row_ok = (key_r[None, :] >= win_r[:, None]) & (key_r[None, :] < win_r[:, None] + NA_KR)
        row_off = np.where(row_ok, key_r[None, :] - r_abs[:, None] + NA_KR - 1, nro)
        t = blocks[:, :, row_off]
        tabs.append(t.transpose(0, 1, 2, 4, 3, 5).reshape(depth, nh, NA_GROUP * GRID_W, NA_WIN * GRID_W))
    return jnp.stack(tabs, axis=2)


def kernel(x, c, ctx, c_ctx, mod_a, mod_b, mod_bias, norm_g, ffn_up, ffn_down, w_in, conv_w,
           gla_decay_w, gla_decay_b, gla_norm_g, na_rpb, w_branch_conv, w_branch_gla, w_branch_na,
           w_out, final_g):
    bsz, seq, d = x.shape
    ctx_len = ctx.shape[1]
    depth = mod_a.shape[0]
    slab = ctx_len + seq
    rows = seq // GRID_W
    assert slab % TM == 0 and d % TN == 0 and seq % ctx_len == 0
    assert rows % NA_GROUP == 0 and rows >= NA_WIN
    assert ctx_len % (2 * GLA_CHUNK) == 0 and seq % (2 * GLA_CHUNK) == 0

    mrows = 16
    cvec = jnp.concatenate([c_ctx[None, :], c, jnp.zeros((mrows - 1 - bsz, d), F32)], axis=0)
    mods = _modulation(cvec, mod_a, mod_b, mod_bias)[:, :1 + bsz].reshape(depth, 1 + bsz, 1, N_MOD * d)

    o_q, _, o_v, o_lr, o_nq = _in_proj_offsets()
    p_runs = [(0, o_q), (o_v, o_lr - o_v), (o_nq, w_in.shape[-1] - o_nq)]
    w_in_t = jnp.swapaxes(w_in, 1, 2)
    wq_t = _qk_weights(w_in_t)
    ffn_up_b = ffn_up.astype(BF16)
    ffn_down_b = ffn_down.astype(BF16).reshape(depth * 2, ffn_down.shape[2], d)
    wb_conv = w_branch_conv.astype(BF16)
    wb_gla = w_branch_gla.astype(BF16)
    wb_na = w_branch_na.astype(BF16)
    w_out_b = w_out.astype(BF16)
    order = _gla_dk_order()
    dw = _take_cols(gla_decay_w, order).reshape(depth, 2, GLA_RANK, GLA_HEADS, GLA_DKP)
    dw = dw.transpose(0, 1, 3, 2, 4)
    wdf = jnp.zeros((depth, GLA_HEADS, 128, GLA_DKP), F32).at[:, :, LR_LANE:LR_LANE + GLA_RANK].set(dw[:, 0])
    wdb = jnp.zeros((depth, GLA_HEADS, 128, GLA_DKP), F32).at[:, :, LR_LANE + GLA_RANK:LR_LANE + 2 * GLA_RANK].set(dw[:, 1])
    wdf = wdf.astype(BF16)
    wdb = wdb.astype(BF16)
    db = _take_cols(gla_decay_b, order).reshape(depth, 2, GLA_HEADS, 1, GLA_DKP)
    ng = gla_norm_g.reshape(depth, GLA_HEADS, 1, GLA_DV)
    cos, sin = _rope_tables(ctx_len, seq)
    na_bias = _na_bias_tables(na_rpb, rows)

    h = jnp.concatenate([ctx, x], axis=1).reshape(bsz * slab, d)
    for l in range(depth):
        m = mods[l]
        nm = functools.partial(_norm_mod, m=m, ctx_len=ctx_len, slab=slab)
        rp = functools.partial(_resid_proj, m=m, ctx_len=ctx_len, slab=slab)
        act = _ffn_up(nm(h, norm_g[l, 0:1], sub=0), ffn_up_b, l, 0)
        h = rp(act, ffn_down_b, 2 * l, h, sub=0, scale=0.5)
        u = nm(h, norm_g[l, 1:2], sub=1)
        p = _proj_runs(u, w_in_t, l, p_runs)
        pq = _proj_runs(u, wq_t, l, [(0, wq_t.shape[1])])
        ya = _conv_branch(p, conv_w[l], bsz, ctx_len, slab)
        yb = _gla_branch(p, pq, cos, sin, wdf[l], wdb[l], db[l, 0], db[l, 1], ng[l], bsz, ctx_len, slab)
        yc = _na_branch(p, na_bias, l, bsz, ctx_len, slab)
        z = _branch_proj(ya, yb, yc, wb_conv, wb_gla, wb_na, l, p)
        h = rp(z, w_out_b, l, h, sub=1, scale=1.0)
        act = _ffn_up(nm(h, norm_g[l, 2:3], sub=2), ffn_up_b, l, 1)
        h = rp(act, ffn_down_b, 2 * l + 1, h, sub=2, scale=0.5)
    return _final_norm(h, final_g.reshape(1, d), bsz, ctx_len, seq)
```

```python
import functools

import numpy as np
import jax
import jax.numpy as jnp
from jax import lax
from jax.experimental import pallas as pl
from jax.experimental.pallas import tpu as pltpu

F32 = jnp.float32
BF16 = jnp.bfloat16

GRID_W = 64
N_MOD = 9
CONV_W = 1024
GLA_HEADS = 4
GLA_DK = 192
GLA_DV = 384
GLA_DKP = 256
GLA_RANK = 16
GLA_TAU = 16.0
GLA_CHUNK = 64
NA_HEADS = 12
NA_HD = 128
NA_W = NA_HEADS * NA_HD
NA_KR = 8
NA_KC = 16
NA_GROUP = 4
NA_WIN = NA_KR + NA_GROUP - 1
ROPE_THETA = 10000.0
EPS = 1e-6

P_H, P_BG, P_CG = 0, 1024, 2048
P_V, P_G = 3072, 4608
P_NQ, P_NK, P_NV = 6144, 7680, 9216
P_GA = 10752
PQ_Q, PQ_K = 0, 1024
LR_LANE = 96

VMEM_LIMIT = 56 * 1024 * 1024
TM = 1152
TM_WIDE = 2 * TM
TN = 512
NORM_ROWS = 256
NORM_SUB_ROWS = 256


def _cparams(n_axes):
    return pltpu.CompilerParams(dimension_semantics=("arbitrary",) * n_axes,
                                vmem_limit_bytes=VMEM_LIMIT)


def _dot(a, b):
    return jnp.dot(a, b, preferred_element_type=F32)


def _dot_nt(a, b):
    return lax.dot_general(a, b, (((1,), (1,)), ((), ())), preferred_element_type=F32)


def _dot_tn(a, b):
    return lax.dot_general(a, b, (((0,), (0,)), ((), ())), preferred_element_type=F32)


def _mod_a_kernel(c_ref, a_ref, o_ref):
    c = c_ref[...]
    s = (c * jax.nn.sigmoid(c)).astype(BF16)
    o_ref[...] = _dot(s, a_ref[...].astype(BF16))


def _mod_b_kernel(t_ref, b_ref, bias_ref, o_ref):
    o_ref[...] = _dot(t_ref[...].astype(BF16), b_ref[...].astype(BF16)) + bias_ref[...]


def _modulation(cvec, mod_a, mod_b, mod_bias):
    depth, d, rank = mod_a.shape
    rows = cvec.shape[0]
    nm = mod_b.shape[-1]
    tn = min(d, 2048)
    assert nm % tn == 0
    t = pl.pallas_call(
        _mod_a_kernel,
        grid=(depth,),
        in_specs=[pl.BlockSpec((rows, d), lambda l: (0, 0)),
                  pl.BlockSpec((None, d, rank), lambda l: (l, 0, 0))],
        out_specs=pl.BlockSpec((None, rows, rank), lambda l: (l, 0, 0)),
        out_shape=jax.ShapeDtypeStruct((depth, rows, rank), F32),
        compiler_params=_cparams(1),
        name="mod_a",
    )(cvec, mod_a)
    return pl.pallas_call(
        _mod_b_kernel,
        grid=(depth, nm // tn),
        in_specs=[pl.BlockSpec((None, rows, rank), lambda l, j: (l, 0, 0)),
                  pl.BlockSpec((None, rank, tn), lambda l, j: (l, 0, j)),
                  pl.BlockSpec((None, 1, tn), lambda l, j: (l, 0, j))],
        out_specs=pl.BlockSpec((None, rows, tn), lambda l, j: (l, 0, j)),
        out_shape=jax.ShapeDtypeStruct((depth, rows, nm), F32),
        compiler_params=_cparams(2),
        name="mod_b",
    )(t, mod_b, mod_bias.reshape(depth, 1, nm))


def _norm_mod_kernel(h_ref, g_ref, shc_ref, scc_ref, shx_ref, scx_ref, o_ref, *, ctx_len, slab):
    tr = h_ref.shape[0]
    sub = NORM_SUB_ROWS
    row0 = (pl.program_id(0) * tr) % slab
    for r in range(0, tr, sub):
        x = h_ref[r:r + sub, :]
        y = x * lax.rsqrt(jnp.mean(x * x, axis=-1, keepdims=True) + EPS) * g_ref[...]
        rows = lax.broadcasted_iota(jnp.int32, (sub, 1), 0) + (row0 + r)
        is_ctx = rows < ctx_len
        sc = jnp.where(is_ctx, scc_ref[...], scx_ref[...])
        sh = jnp.where(is_ctx, shc_ref[...], shx_ref[...])
        o_ref[r:r + sub, :] = (y * (1.0 + sc) + sh).astype(o_ref.dtype)


def _norm_mod(h, g, m, sub, ctx_len, slab):
    mrows, d = h.shape
    tr = NORM_ROWS
    assert slab % tr == 0
    ctx_row = lambda c: (lambda j: (0, 0, c))
    batch_row = lambda c: (lambda j: (1 + (j * tr) // slab, 0, c))
    return pl.pallas_call(
        functools.partial(_norm_mod_kernel, ctx_len=ctx_len, slab=slab),
        grid=(mrows // tr,),
        in_specs=[pl.BlockSpec((tr, d), lambda j: (j, 0)),
                  pl.BlockSpec((1, d), lambda j: (0, 0)),
                  pl.BlockSpec((None, 1, d), ctx_row(3 * sub)),
                  pl.BlockSpec((None, 1, d), ctx_row(3 * sub + 1)),
                  pl.BlockSpec((None, 1, d), batch_row(3 * sub)),
                  pl.BlockSpec((None, 1, d), batch_row(3 * sub + 1))],
        out_specs=pl.BlockSpec((tr, d), lambda j: (j, 0)),
        out_shape=jax.ShapeDtypeStruct((mrows, d), BF16),
        compiler_params=_cparams(1),
        name="norm_mod",
    )(h, g, m, m, m, m)


def _final_norm_kernel(h_ref, g_ref, o_ref):
    x = h_ref[...]
    o_ref[...] = x * lax.rsqrt(jnp.mean(x * x, axis=-1, keepdims=True) + EPS) * g_ref[...]


def _final_norm(h, g, bsz, ctx_len, seq):
    d = h.shape[-1]
    slab = ctx_len + seq
    tr = ctx_len
    return pl.pallas_call(
        _final_norm_kernel,
        grid=(bsz, seq // tr),
        in_specs=[pl.BlockSpec((None, tr, d), lambda b, j: (b, 1 + j, 0)),
                  pl.BlockSpec((1, d), lambda b, j: (0, 0))],
        out_specs=pl.BlockSpec((None, tr, d), lambda b, j: (b, j, 0)),
        out_shape=jax.ShapeDtypeStruct((bsz, seq, d), F32),
        compiler_params=_cparams(2),
        name="final_norm",
    )(h.reshape(bsz, slab, d), g)


def _proj_t_kernel(a_ref, wt_ref, o_ref):
    o_ref[...] = _dot_nt(a_ref[...], wt_ref[0].astype(BF16)).astype(o_ref.dtype)


def _proj_runs(a, wt, l, runs):
    mrows, k = a.shape
    assert all(s % 8 == 0 and n % TN == 0 for s, n in runs)
    starts = np.concatenate([s + TN * np.arange(n // TN) for s, n in runs])
    first = int(starts[0])
    step_at = [int(i) for i in np.nonzero(np.diff(starts) != TN)[0] + 1]
    jumps = [int(starts[i] - starts[i - 1] - TN) for i in step_at]

    def wrow(j):
        r = j * TN + first
        for at, jump in zip(step_at, jumps):
            r = r + jnp.where(j >= at, jump, 0)
        return pl.multiple_of(r, 8)

    tm = TM_WIDE
    return pl.pallas_call(
        _proj_t_kernel,
        grid=(mrows // tm, len(starts)),
        in_specs=[pl.BlockSpec((tm, k), lambda i, j: (i, 0), pipeline_mode=pl.Buffered(1)),
                  pl.BlockSpec((pl.Element(1), pl.Element(TN), pl.Element(k)), lambda i, j: (l, wrow(j), 0))],
        out_specs=pl.BlockSpec((tm, TN), lambda i, j: (i, j)),
        out_shape=jax.ShapeDtypeStruct((mrows, len(starts) * TN), BF16),
        compiler_params=_cparams(2),
        name="in_proj_runs",
    )(a, wt)


def _swiglu_kernel(u_ref, wa_ref, wb_ref, o_ref):
    u = u_ref[...]
    a = _dot(u, wa_ref[...])
    b = _dot(u, wb_ref[...])
    o_ref[...] = (a * jax.nn.sigmoid(a) * b).astype(o_ref.dtype)


def _ffn_up(u, w_up, l, s):
    mrows, k = u.shape
    dff = w_up.shape[-1] // 2
    nb = dff // TN
    return pl.pallas_call(
        _swiglu_kernel,
        grid=(mrows // TM, nb),
        in_specs=[pl.BlockSpec((TM, k), lambda i, j: (i, 0)),
                  pl.BlockSpec((None, None, k, TN), lambda i, j: (l, s, 0, j)),
                  pl.BlockSpec((None, None, k, TN), lambda i, j: (l, s, 0, nb + j))],
        out_specs=pl.BlockSpec((TM, TN), lambda i, j: (i, j)),
        out_shape=jax.ShapeDtypeStruct((mrows, dff), BF16),
        compiler_params=_cparams(2),
        name="ffn_up",
    )(u, w_up, w_up)


def _resid_kernel(a_ref, w_ref, h_ref, gc_ref, gx_ref, o_ref, *, scale, ctx_len, slab):
    tm = a_ref.shape[0]
    acc = _dot(a_ref[...], w_ref[...])
    row0 = (pl.program_id(0) * tm) % slab
    rows = lax.broadcasted_iota(jnp.int32, (tm, 1), 0) + row0
    gate = jnp.where(rows < ctx_len, gc_ref[...], gx_ref[...])
    o_ref[...] = h_ref[...] + (scale * gate) * acc


def _resid_proj(a, w, widx, h, m, sub, scale, ctx_len, slab):
    mrows, k = a.shape
    n = w.shape[-1]
    g0 = (3 * sub + 2) * (n // TN)
    return pl.pallas_call(
        functools.partial(_resid_kernel, scale=scale, ctx_len=ctx_len, slab=slab),
        grid=(mrows // TM_WIDE, n // TN),
        in_specs=[pl.BlockSpec((TM_WIDE, k), lambda i, j: (i, 0), pipeline_mode=pl.Buffered(1)),
                  pl.BlockSpec((None, k, TN), lambda i, j: (widx, 0, j)),
                  pl.BlockSpec((TM_WIDE, TN), lambda i, j: (i, j)),
                  pl.BlockSpec((None, 1, TN), lambda i, j: (0, 0, g0 + j)),
                  pl.BlockSpec((None, 1, TN), lambda i, j: (1 + (i * TM_WIDE) // slab, 0, g0 + j))],
        out_specs=pl.BlockSpec((TM_WIDE, TN), lambda i, j: (i, j)),
        out_shape=jax.ShapeDtypeStruct((mrows, n), F32),
        input_output_aliases={2: 0},
        compiler_params=_cparams(2),
        name="resid_proj",
    )(a, w, h, m, m)


def _branch_kernel(a_ref, b_ref, c_ref, pa_ref, pb_ref, pc_ref, ga_ref, gb_ref, gc_ref, o_ref):
    ya = _dot(a_ref[...], pa_ref[...])
    yb = _dot(b_ref[...], pb_ref[...])
    yc = _dot(c_ref[...], pc_ref[...])
    z = (jax.nn.sigmoid(ga_ref[...].astype(F32)) * ya
         + jax.nn.sigmoid(gb_ref[...].astype(F32)) * yb
         + jax.nn.sigmoid(gc_ref[...].astype(F32)) * yc)
    o_ref[...] = z.astype(o_ref.dtype)


def _branch_proj(a, b, c, pa, pb, pc, l, p):
    mrows = a.shape[0]
    d = pa.shape[-1]
    tn = TN
    g0 = P_GA // tn
    nb = d // tn
    return pl.pallas_call(
        _branch_kernel,
        grid=(mrows // TM, nb),
        in_specs=[pl.BlockSpec((TM, a.shape[1]), lambda i, j: (i, 0)),
                  pl.BlockSpec((TM, b.shape[1]), lambda i, j: (i, 0)),
                  pl.BlockSpec((TM, c.shape[1]), lambda i, j: (i, 0)),
                  pl.BlockSpec((None, pa.shape[1], tn), lambda i, j: (l, 0, j)),
                  pl.BlockSpec((None, pb.shape[1], tn), lambda i, j: (l, 0, j)),
                  pl.BlockSpec((None, pc.shape[1], tn), lambda i, j: (l, 0, j)),
                  pl.BlockSpec((TM, tn), lambda i, j: (i, g0 + j)),
                  pl.BlockSpec((TM, tn), lambda i, j: (i, g0 + nb + j)),
                  pl.BlockSpec((TM, tn), lambda i, j: (i, g0 + 2 * nb + j))],
        out_specs=pl.BlockSpec((TM, tn), lambda i, j: (i, j)),
        out_shape=jax.ShapeDtypeStruct((mrows, d), BF16),
        compiler_params=_cparams(2),
        name="branch_proj",
    )(a, b, c, pa, pb, pc, p, p, p)


def _conv_kernel(h_ref, bg_ref, cg_ref, w_ref, o_ref, *, ctx_len):
    n = h_ref.shape[0]
    x = cg_ref[...].astype(F32) * h_ref[...].astype(F32)
    row = lax.broadcasted_iota(jnp.int32, (n, 1), 0)
    first = (row == 0) | (row == ctx_len)
    last = (row == ctx_len - 1) | (row == n - 1)
    prev = jnp.where(first, 0.0, pltpu.roll(x, 1, axis=0))
    nxt = jnp.where(last, 0.0, pltpu.roll(x, n - 1, axis=0))
    w = w_ref[...]
    y = prev * w[0:1, :] + x * w[1:2, :] + nxt * w[2:3, :]
    o_ref[...] = (bg_ref[...].astype(F32) * y).astype(o_ref.dtype)


def _conv_branch(p, conv_w, bsz, ctx_len, slab):
    tc = 256
    nb = CONV_W // tc
    return pl.pallas_call(
        functools.partial(_conv_kernel, ctx_len=ctx_len),
        grid=(bsz, nb),
        in_specs=[pl.BlockSpec((slab, tc), lambda b, j: (b, P_H // tc + j)),
                  pl.BlockSpec((slab, tc), lambda b, j: (b, P_BG // tc + j)),
                  pl.BlockSpec((slab, tc), lambda b, j: (b, P_CG // tc + j)),
                  pl.BlockSpec((conv_w.shape[0], tc), lambda b, j: (0, j))],
        out_specs=pl.BlockSpec((slab, tc), lambda b, j: (b, j)),
        out_shape=jax.ShapeDtypeStruct((bsz * slab, CONV_W), BF16),
        compiler_params=_cparams(2),
        name="conv_branch",
    )(p, p, p, conv_w)


def _log_sigmoid(x):
    return jnp.minimum(x, 0.0) - jnp.log(1.0 + jnp.exp(-jnp.abs(x)))


def _gla_kernel(q_ref, k_ref, v_ref, g_ref, lr_ref, cos_ref, sin_ref, wdf_ref, wdb_ref,
                bdf_ref, bdb_ref, ng_ref, o_ref, qd_s, q2_s, ki_s, ke_s, dec_s, oacc_s, st_s,
                *, ctx_len):
    n = q_ref.shape[0]
    pair = 2 * GLA_CHUNK
    npair = n // pair
    cpair = ctx_len // pair
    half = GLA_DKP // 2
    cos = cos_ref[...]
    sin = sin_ref[...]

    def rope(x):
        x1 = x[:, :half]
        x2 = x[:, half:]
        return jnp.concatenate([x1 * cos - x2 * sin, x1 * sin + x2 * cos], axis=1)

    qr = rope(q_ref[...].astype(F32)) * (GLA_DK ** -0.5)
    kr = rope(k_ref[...].astype(F32))
    lr = lr_ref[...]
    ridx = lax.broadcasted_iota(jnp.int32, (n, 1), 0) % GLA_CHUNK
    pi = lax.broadcasted_iota(jnp.int32, (pair, pair), 0)
    pj = lax.broadcasted_iota(jnp.int32, (pair, pair), 1)
    same = (pi < GLA_CHUNK) == (pj < GLA_CHUNK)

    def spread(t4):
        return jnp.broadcast_to(t4, (npair, 2, GLA_CHUNK, GLA_DKP)).reshape(n, GLA_DKP)

    def chunk_sums(la):
        cum = la
        for sft in (1, 2, 4, 8, 16, 32):
            cum = cum + jnp.where(ridx >= sft, pltpu.roll(cum, sft, axis=0), 0.0)
        tot4 = cum.reshape(npair, 2, GLA_CHUNK, GLA_DKP)[:, :, GLA_CHUNK - 1:GLA_CHUNK, :]
        return cum, tot4

    def stage(cum, tot4, forward):
        lo, hi = tot4[:, 0:1], tot4[:, 1:2]
        zero = jnp.zeros_like(lo)
        if forward:
            q_add = spread(jnp.concatenate([zero, lo], axis=1))
            k_add = spread(jnp.concatenate([hi, zero], axis=1))
        else:
            q_add = spread(jnp.concatenate([hi, zero], axis=1))
            k_add = spread(jnp.concatenate([zero, lo], axis=1))
        qd_s[...] = (qr * jnp.exp(cum)).astype(BF16)
        q2_s[...] = (qr * jnp.exp(cum + q_add)).astype(BF16)
        ki_s[...] = (kr * jnp.exp(-cum)).astype(BF16)
        ke_s[...] = (kr * jnp.exp(spread(tot4) - cum + k_add)).astype(BF16)
        dec_s[...] = jnp.exp(lo[:, 0] + hi[:, 0])

    def scan(pair_of, steps, diag, off, first):
        def body(i, carry):
            c = pair_of(i)
            rows = pl.ds(pl.multiple_of(c * pair, pair), pair)
            q2 = q2_s[rows, :]
            ki = ki_s[rows, :]
            vv = v_ref[rows, :]
            att = jnp.where(diag, _dot_nt(qd_s[rows, :], ki), jnp.where(off, _dot_nt(q2, ki), 0.0))
            st = st_s[...]
            o = _dot(att.astype(BF16), vv) + _dot_nt(q2, st.astype(BF16))
            if first:
                oacc_s[rows, :] = o
            else:
                oacc_s[rows, :] += o
            st_s[...] = st * dec_s[c] + _dot_tn(vv, ke_s[rows, :])
            return carry
        lax.fori_loop(0, steps, body, 0, unroll=2)

    la = _log_sigmoid(_dot(lr, wdf_ref[...]) + bdf_ref[...]) / GLA_TAU
    cum, tot4 = chunk_sums(la)
    stage(cum, tot4, True)
    st_s[...] = jnp.zeros_like(st_s)
    scan(lambda i: i, npair, same & (pi >= pj), pi > pj, True)

    la = _log_sigmoid(_dot(lr, wdb_ref[...]) + bdb_ref[...]) / GLA_TAU
    cum, tot4 = chunk_sums(la)
    stage(spread(tot4) - cum + la, tot4, False)
    st_s[...] = jnp.zeros_like(st_s)
    scan(lambda i: cpair - 1 - i, cpair, same & (pi <= pj), pi < pj, False)
    scan(lambda i: npair - 1 - i, npair - cpair, same & (pi <= pj), pi < pj, False)

    o = oacc_s[...]
    o = o * lax.rsqrt(jnp.mean(o * o, axis=-1, keepdims=True) + EPS) * ng_ref[...]
    g = g_ref[...].astype(F32)
    o_ref[...] = (o * (g * jax.nn.sigmoid(g))).astype(o_ref.dtype)


def _gla_branch(p1, pq, cos, sin, wdf, wdb, bdf, bdb, ng, bsz, ctx_len, slab):
    npair = slab // (2 * GLA_CHUNK)
    head = lambda b, h: (h, 0, 0)
    return pl.pallas_call(
        functools.partial(_gla_kernel, ctx_len=ctx_len),
        grid=(bsz, GLA_HEADS),
        in_specs=[pl.BlockSpec((slab, GLA_DKP), lambda b, h: (b, PQ_Q // GLA_DKP + h)),
                  pl.BlockSpec((slab, GLA_DKP), lambda b, h: (b, PQ_K // GLA_DKP + h)),
                  pl.BlockSpec((slab, GLA_DV), lambda b, h: (b, P_V // GLA_DV + h)),
                  pl.BlockSpec((slab, GLA_DV), lambda b, h: (b, P_G // GLA_DV + h)),
                  pl.BlockSpec((slab, 128), lambda b, h: (b, PQ_Q // 128)),
                  pl.BlockSpec((slab, GLA_DKP // 2), lambda b, h: (0, 0)),
                  pl.BlockSpec((slab, GLA_DKP // 2), lambda b, h: (0, 0)),
                  pl.BlockSpec((None, 128, GLA_DKP), head),
                  pl.BlockSpec((None, 128, GLA_DKP), head),
                  pl.BlockSpec((None, 1, GLA_DKP), head),
                  pl.BlockSpec((None, 1, GLA_DKP), head),
                  pl.BlockSpec((None, 1, GLA_DV), head)],
        out_specs=pl.BlockSpec((slab, GLA_DV), lambda b, h: (b, h)),
        out_shape=jax.ShapeDtypeStruct((bsz * slab, GLA_HEADS * GLA_DV), BF16),
        scratch_shapes=[pltpu.VMEM((slab, GLA_DKP), BF16),
                        pltpu.VMEM((slab, GLA_DKP), BF16),
                        pltpu.VMEM((slab, GLA_DKP), BF16),
                        pltpu.VMEM((slab, GLA_DKP), BF16),
                        pltpu.VMEM((npair, 1, GLA_DKP), F32),
                        pltpu.VMEM((slab, GLA_DV), F32),
                        pltpu.VMEM((GLA_DV, GLA_DKP), F32)],
        compiler_params=_cparams(2),
        name="gla_branch",
    )(pq, pq, p1, p1, pq, cos, sin, wdf, wdb, bdf, bdb, ng)


def _na_group_base(g, rows):
    return min(max(g * NA_GROUP - NA_KR // 2, 0), rows - NA_WIN)


def _na_build_bias(rpb_ref, bias_ref, rows):
    w = GRID_W
    qc = lax.broadcasted_iota(jnp.int32, (w, w), 0)
    kc = lax.broadcasted_iota(jnp.int32, (w, w), 1)
    win = jnp.clip(qc - NA_KC // 2, 0, w - NA_KC)
    col_ok = (kc >= win) & (kc < win + NA_KC)
    neg = jnp.full((w, w), -jnp.inf, F32)
    blocks = []
    for ro in range(2 * NA_KR - 1):
        row = jnp.broadcast_to(rpb_ref[ro:ro + 1, :], (w, 128))
        band = pltpu.roll(row, 128 - (NA_KC - 1), axis=1, stride=1, stride_axis=0)
        blocks.append(jnp.where(col_ok, band[:, :w], neg))
    ngroups = rows // NA_GROUP
    for kind, g in enumerate((0, 1, ngroups - 1)):
        base = _na_group_base(g, rows)
        for qr in range(NA_GROUP):
            r_abs = g * NA_GROUP + qr
            win_r = min(max(r_abs - NA_KR // 2, 0), rows - NA_KR)
            pieces = []
            for kr in range(NA_WIN):
                key_r = base + kr
                inside = win_r <= key_r < win_r + NA_KR
                pieces.append(blocks[key_r - r_abs + NA_KR - 1] if inside else neg)
            bias_ref[kind, qr * w:(qr + 1) * w, :] = jnp.concatenate(pieces, axis=1)


def _na_kernel(q_ref, k_ref, v_ref, rpb_ref, o_ref, bias_ref, *, ctx_len, rows):
    scale = NA_HD ** -0.5
    gq = NA_GROUP * GRID_W
    nwin = NA_WIN * GRID_W
    kc = k_ref[0:ctx_len, :]
    vc = v_ref[0:ctx_len, :]

    @pl.when(pl.program_id(1) == 0)
    def _():
        _na_build_bias(rpb_ref, bias_ref, rows)

    s = _dot_nt(q_ref[0:ctx_len, :], kc) * scale
    e = jnp.exp(s - jnp.max(s, axis=-1, keepdims=True))
    pr = e / jnp.sum(e, axis=-1, keepdims=True)
    o_ref[0:ctx_len, :] = _dot(pr.astype(BF16), vc).astype(o_ref.dtype)

    ngroups = rows // NA_GROUP
    for g in range(ngroups):
        kind = 0 if g == 0 else (2 if g == ngroups - 1 else 1)
        q0 = ctx_len + g * gq
        k0 = ctx_len + _na_group_base(g, rows) * GRID_W
        qg = q_ref[q0:q0 + gq, :]
        s_loc = _dot_nt(qg, k_ref[k0:k0 + nwin, :]) * scale + bias_ref[kind]
        s_ctx = _dot_nt(qg, kc) * scale
        m = jnp.maximum(jnp.max(s_loc, axis=-1, keepdims=True), jnp.max(s_ctx, axis=-1, keepdims=True))
        e_loc = jnp.exp(s_loc - m)
        e_ctx = jnp.exp(s_ctx - m)
        den = jnp.sum(e_loc, axis=-1, keepdims=True) + jnp.sum(e_ctx, axis=-1, keepdims=True)
        o = _dot(e_loc.astype(BF16), v_ref[k0:k0 + nwin, :]) + _dot(e_ctx.astype(BF16), vc)
        o_ref[q0:q0 + gq, :] = (o / den).astype(o_ref.dtype)


def _na_branch(p, rpb, l, bsz, ctx_len, slab):
    rows = (slab - ctx_len) // GRID_W
    return pl.pallas_call(
        functools.partial(_na_kernel, ctx_len=ctx_len, rows=rows),
        grid=(NA_HEADS, bsz),
        in_specs=[pl.BlockSpec((slab, NA_HD), lambda h, b: (b, P_NQ // NA_HD + h)),
                  pl.BlockSpec((slab, NA_HD), lambda h, b: (b, P_NK // NA_HD + h)),
                  pl.BlockSpec((slab, NA_HD), lambda h, b: (b, P_NV // NA_HD + h)),
                  pl.BlockSpec((None, None) + rpb.shape[2:], lambda h, b: (l, h, 0, 0))],
        out_specs=pl.BlockSpec((slab, NA_HD), lambda h, b: (b, h)),
        out_shape=jax.ShapeDtypeStruct((bsz * slab, NA_W), BF16),
        scratch_shapes=[pltpu.VMEM((3, NA_GROUP * GRID_W, NA_WIN * GRID_W), F32)],
        compiler_params=_cparams(2),
        name="na_branch",
    )(p, p, p, rpb)


def _gla_dk_order():
    q = GLA_DK // 4
    one = np.full((GLA_DKP,), -1, np.int64)
    one[0:q] = np.arange(0, q)
    one[q:2 * q] = np.arange(2 * q, 3 * q)
    one[GLA_DKP // 2:GLA_DKP // 2 + q] = np.arange(q, 2 * q)
    one[GLA_DKP // 2 + q:GLA_DKP // 2 + 2 * q] = np.arange(3 * q, 4 * q)
    out = np.concatenate([np.where(one >= 0, one + h * GLA_DK, -1) for h in range(GLA_HEADS)])
    return out


def _take_cols(w, cols, axis=-1):
    axis = axis % w.ndim
    pieces = []
    i = 0
    n = len(cols)
    while i < n:
        j = i + 1
        if cols[i] < 0:
            while j < n and cols[j] < 0:
                j += 1
            pieces.append(jnp.zeros(w.shape[:axis] + (j - i,) + w.shape[axis + 1:], w.dtype))
        else:
            while j < n and cols[j] == cols[j - 1] + 1:
                j += 1
            pieces.append(lax.slice_in_dim(w, int(cols[i]), int(cols[i]) + (j - i), axis=axis))
        i = j
    return jnp.concatenate(pieces, axis=axis)


def _in_proj_offsets():
    o_q = 3 * CONV_W
    o_k = o_q + GLA_HEADS * GLA_DK
    o_v = o_k + GLA_HEADS * GLA_DK
    o_lr = o_v + 2 * GLA_HEADS * GLA_DV
    o_nq = o_lr + 2 * GLA_RANK
    return o_q, o_k, o_v, o_lr, o_nq


def _qk_weights(w_in_t):
    o_q, o_k, o_v, o_lr, o_nq = _in_proj_offsets()
    order = _gla_dk_order()
    q_order = np.where(order >= 0, order + o_q, -1)
    q_order[LR_LANE:LR_LANE + 2 * GLA_RANK] = o_lr + np.arange(2 * GLA_RANK)
    k_order = np.where(order >= 0, order + o_k, -1)
    return _take_cols(w_in_t, np.concatenate([q_order, k_order]), axis=1)


def _rope_tables(ctx_len, seq):
    q = GLA_DK // 4
    pos = jnp.arange(seq)
    freq = ROPE_THETA ** (-jnp.arange(q, dtype=F32) / q)
    ang_r = (pos // GRID_W).astype(F32)[:, None] * freq[None, :]
    ang_c = (pos % GRID_W).astype(F32)[:, None] * freq[None, :]
    pad = GLA_DKP // 2 - 2 * q
    cos = jnp.concatenate([jnp.cos(ang_r), jnp.cos(ang_c), jnp.ones((seq, pad), F32)], axis=1)
    sin = jnp.concatenate([jnp.sin(ang_r), jnp.sin(ang_c), jnp.zeros((seq, pad), F32)], axis=1)
    cos = jnp.concatenate([jnp.ones((ctx_len, GLA_DKP // 2), F32), cos], axis=0)
    sin = jnp.concatenate([jnp.zeros((ctx_len, GLA_DKP // 2), F32), sin], axis=0)
    return cos, sin


def kernel(x, c, ctx, c_ctx, mod_a, mod_b, mod_bias, norm_g, ffn_up, ffn_down, w_in, conv_w,
           gla_decay_w, gla_decay_b, gla_norm_g, na_rpb, w_branch_conv, w_branch_gla, w_branch_na,
           w_out, final_g):
    bsz, seq, d = x.shape
    ctx_len = ctx.shape[1]
    depth = mod_a.shape[0]
    slab = ctx_len + seq
    rows = seq // GRID_W
    assert slab % TM == 0 and d % TN == 0 and seq % ctx_len == 0
    assert rows % NA_GROUP == 0 and rows >= NA_WIN
    assert ctx_len % (2 * GLA_CHUNK) == 0 and seq % (2 * GLA_CHUNK) == 0

    mrows = 16
    cvec = jnp.concatenate([c_ctx[None, :], c, jnp.zeros((mrows - 1 - bsz, d), F32)], axis=0)
    mods = _modulation(cvec, mod_a, mod_b, mod_bias)[:, :1 + bsz].reshape(depth, 1 + bsz, 1, N_MOD * d)

    o_q, _, o_v, o_lr, o_nq = _in_proj_offsets()
    p_runs = [(0, o_q), (o_v, o_lr - o_v), (o_nq, w_in.shape[-1] - o_nq)]
    w_in_t = jnp.swapaxes(w_in, 1, 2)
    wq_t = _qk_weights(w_in_t)
    ffn_up_b = ffn_up.astype(BF16)
    ffn_down_b = ffn_down.astype(BF16).reshape(depth * 2, ffn_down.shape[2], d)
    wb_conv = w_branch_conv.astype(BF16)
    wb_gla = w_branch_gla.astype(BF16)
    wb_na = w_branch_na.astype(BF16)
    w_out_b = w_out.astype(BF16)
    order = _gla_dk_order()
    dw = _take_cols(gla_decay_w, order).reshape(depth, 2, GLA_RANK, GLA_HEADS, GLA_DKP)
    dw = dw.transpose(0, 1, 3, 2, 4)
    wdf = jnp.zeros((depth, GLA_HEADS, 128, GLA_DKP), F32).at[:, :, LR_LANE:LR_LANE + GLA_RANK].set(dw[:, 0])
    wdb = jnp.zeros((depth, GLA_HEADS, 128, GLA_DKP), F32).at[:, :, LR_LANE + GLA_RANK:LR_LANE + 2 * GLA_RANK].set(dw[:, 1])
    wdf = wdf.astype(BF16)
    wdb = wdb.astype(BF16)
    db = _take_cols(gla_decay_b, order).reshape(depth, 2, GLA_HEADS, 1, GLA_DKP)
    ng = gla_norm_g.reshape(depth, GLA_HEADS, 1, GLA_DV)
    cos, sin = _rope_tables(ctx_len, seq)
    rpb_pad = jnp.pad(na_rpb, ((0, 0), (0, 0), (0, 1), (0, 128 - na_rpb.shape[-1])))

    h = jnp.concatenate([ctx, x], axis=1).reshape(bsz * slab, d)
    for l in range(depth):
        m = mods[l]
        nm = functools.partial(_norm_mod, m=m, ctx_len=ctx_len, slab=slab)
        rp = functools.partial(_resid_proj, m=m, ctx_len=ctx_len, slab=slab)
        act = _ffn_up(nm(h, norm_g[l, 0:1], sub=0), ffn_up_b, l, 0)
        h = rp(act, ffn_down_b, 2 * l, h, sub=0, scale=0.5)
        u = nm(h, norm_g[l, 1:2], sub=1)
        p = _proj_runs(u, w_in_t, l, p_runs)
        pq = _proj_runs(u, wq_t, l, [(0, wq_t.shape[1])])
        ya = _conv_branch(p, conv_w[l], bsz, ctx_len, slab)
        yb = _gla_branch(p, pq, cos, sin, wdf[l], wdb[l], db[l, 0], db[l, 1], ng[l], bsz, ctx_len, slab)
        yc = _na_branch(p, rpb_pad, l, bsz, ctx_len, slab)
        z = _branch_proj(ya, yb, yc, wb_conv, wb_gla, wb_na, l, p)
        h = rp(z, w_out_b, l, h, sub=1, scale=1.0)
        act = _ffn_up(nm(h, norm_g[l, 2:3], sub=2), ffn_up_b, l, 1)
        h = rp(act, ffn_down_b, 2 * l + 1, h, sub=2, scale=0.5)
    return _final_norm(h, final_g.reshape(1, d), bsz, ctx_len, seq)
```

```python
import functools

import numpy as np
import jax
import jax.numpy as jnp
from jax import lax
from jax.experimental import pallas as pl
from jax.experimental.pallas import tpu as pltpu

F32 = jnp.float32
BF16 = jnp.bfloat16

GRID_W = 64
N_MOD = 9
CONV_W = 1024
GLA_HEADS = 4
GLA_DK = 192
GLA_DV = 384
GLA_DKP = 256
GLA_RANK = 16
GLA_TAU = 16.0
GLA_CHUNK = 64
NA_HEADS = 12
NA_HD = 128
NA_W = NA_HEADS * NA_HD
NA_KR = 8
NA_KC = 16
NA_GROUP = 4
NA_WIN = NA_KR + NA_GROUP - 1
ROPE_THETA = 10000.0
EPS = 1e-6
LOG2E = 1.4426950408889634

P_H, P_BG, P_CG = 0, 1024, 2048
P_V, P_G = 3072, 4608
P_NQ, P_NK, P_NV = 6144, 7680, 9216
P_GA = 10752
PQ_Q, PQ_K = 0, 1024
LR_LANE = 96

VMEM_LIMIT = 56 * 1024 * 1024
TM = 1152
TM_WIDE = 2 * TM
TN = 512
NORM_ROWS = 256


def _cparams(n_axes):
    return pltpu.CompilerParams(dimension_semantics=("arbitrary",) * n_axes,
                                vmem_limit_bytes=VMEM_LIMIT)


def _dot(a, b):
    return jnp.dot(a, b, preferred_element_type=F32)


def _dot_nt(a, b):
    return lax.dot_general(a, b, (((1,), (1,)), ((), ())), preferred_element_type=F32)


def _dot_tn(a, b):
    return lax.dot_general(a, b, (((0,), (0,)), ((), ())), preferred_element_type=F32)


def _mod_a_kernel(c_ref, a_ref, o_ref):
    c = c_ref[...]
    s = (c * jax.nn.sigmoid(c)).astype(BF16)
    o_ref[...] = _dot(s, a_ref[...].astype(BF16))


def _mod_b_kernel(t_ref, b_ref, bias_ref, o_ref):
    o_ref[...] = _dot(t_ref[...].astype(BF16), b_ref[...].astype(BF16)) + bias_ref[...]


def _modulation(cvec, mod_a, mod_b, mod_bias):
    depth, d, rank = mod_a.shape
    rows = cvec.shape[0]
    nm = mod_b.shape[-1]
    tn = min(d, 2048)
    assert nm % tn == 0
    t = pl.pallas_call(
        _mod_a_kernel,
        grid=(depth,),
        in_specs=[pl.BlockSpec((rows, d), lambda l: (0, 0)),
                  pl.BlockSpec((None, d, rank), lambda l: (l, 0, 0))],
        out_specs=pl.BlockSpec((None, rows, rank), lambda l: (l, 0, 0)),
        out_shape=jax.ShapeDtypeStruct((depth, rows, rank), F32),
        compiler_params=_cparams(1),
        name="mod_a",
    )(cvec, mod_a)
    return pl.pallas_call(
        _mod_b_kernel,
        grid=(depth, nm // tn),
        in_specs=[pl.BlockSpec((None, rows, rank), lambda l, j: (l, 0, 0)),
                  pl.BlockSpec((None, rank, tn), lambda l, j: (l, 0, j)),
                  pl.BlockSpec((None, 1, tn), lambda l, j: (l, 0, j))],
        out_specs=pl.BlockSpec((None, rows, tn), lambda l, j: (l, 0, j)),
        out_shape=jax.ShapeDtypeStruct((depth, rows, nm), F32),
        compiler_params=_cparams(2),
        name="mod_b",
    )(t, mod_b, mod_bias.reshape(depth, 1, nm))


def _norm_mod_kernel(h_ref, g_ref, shc_ref, scc_ref, shx_ref, scx_ref, o_ref, *, ctx_len, slab):
    tr = h_ref.shape[0]
    is_ctx = (pl.program_id(0) * tr) % slab < ctx_len
    sc = jnp.where(is_ctx, scc_ref[...], scx_ref[...])
    sh = jnp.where(is_ctx, shc_ref[...], shx_ref[...])
    x = h_ref[...]
    y = x * lax.rsqrt(jnp.mean(x * x, axis=-1, keepdims=True) + EPS) * g_ref[...]
    o_ref[...] = (y * (1.0 + sc) + sh).astype(o_ref.dtype)


def _norm_mod(h, g, m, sub, ctx_len, slab):
    mrows, d = h.shape
    tr = NORM_ROWS
    assert slab % tr == 0 and ctx_len % tr == 0
    ctx_row = lambda c: (lambda j: (0, 0, c))
    batch_row = lambda c: (lambda j: (1 + (j * tr) // slab, 0, c))
    return pl.pallas_call(
        functools.partial(_norm_mod_kernel, ctx_len=ctx_len, slab=slab),
        grid=(mrows // tr,),
        in_specs=[pl.BlockSpec((tr, d), lambda j: (j, 0)),
                  pl.BlockSpec((1, d), lambda j: (0, 0)),
                  pl.BlockSpec((None, 1, d), ctx_row(3 * sub)),
                  pl.BlockSpec((None, 1, d), ctx_row(3 * sub + 1)),
                  pl.BlockSpec((None, 1, d), batch_row(3 * sub)),
                  pl.BlockSpec((None, 1, d), batch_row(3 * sub + 1))],
        out_specs=pl.BlockSpec((tr, d), lambda j: (j, 0)),
        out_shape=jax.ShapeDtypeStruct((mrows, d), BF16),
        compiler_params=_cparams(1),
        name="norm_mod",
    )(h, g, m, m, m, m)


def _final_norm_kernel(h_ref, g_ref, o_ref):
    x = h_ref[...]
    o_ref[...] = x * lax.rsqrt(jnp.mean(x * x, axis=-1, keepdims=True) + EPS) * g_ref[...]


def _final_norm(h, g, bsz, ctx_len, seq):
    d = h.shape[-1]
    slab = ctx_len + seq
    tr = ctx_len
    return pl.pallas_call(
        _final_norm_kernel,
        grid=(bsz, seq // tr),
        in_specs=[pl.BlockSpec((None, tr, d), lambda b, j: (b, 1 + j, 0)),
                  pl.BlockSpec((1, d), lambda b, j: (0, 0))],
        out_specs=pl.BlockSpec((None, tr, d), lambda b, j: (b, j, 0)),
        out_shape=jax.ShapeDtypeStruct((bsz, seq, d), F32),
        compiler_params=_cparams(2),
        name="final_norm",
    )(h.reshape(bsz, slab, d), g)


def _proj_t_kernel(a_ref, wt_ref, o_ref):
    o_ref[...] = _dot_nt(a_ref[...], wt_ref[0].astype(BF16)).astype(o_ref.dtype)


def _proj_runs(a, wt, l, runs):
    mrows, k = a.shape
    assert all(s % 8 == 0 and n % TN == 0 for s, n in runs)
    starts = np.concatenate([s + TN * np.arange(n // TN) for s, n in runs])
    first = int(starts[0])
    step_at = [int(i) for i in np.nonzero(np.diff(starts) != TN)[0] + 1]
    jumps = [int(starts[i] - starts[i - 1] - TN) for i in step_at]

    def wrow(j):
        r = j * TN + first
        for at, jump in zip(step_at, jumps):
            r = r + jnp.where(j >= at, jump, 0)
        return pl.multiple_of(r, 8)

    wide = len(starts) >= 16
    tm = TM_WIDE if wide else TM
    a_spec = (pl.BlockSpec((tm, k), lambda i, j: (i, 0), pipeline_mode=pl.Buffered(1)) if wide
              else pl.BlockSpec((tm, k), lambda i, j: (i, 0)))
    return pl.pallas_call(
        _proj_t_kernel,
        grid=(mrows // tm, len(starts)),
        in_specs=[a_spec,
                  pl.BlockSpec((pl.Element(1), pl.Element(TN), pl.Element(k)), lambda i, j: (l, wrow(j), 0))],
        out_specs=pl.BlockSpec((tm, TN), lambda i, j: (i, j)),
        out_shape=jax.ShapeDtypeStruct((mrows, len(starts) * TN), BF16),
        compiler_params=_cparams(2),
        name="in_proj_runs",
    )(a, wt)


def _swiglu_kernel(u_ref, wa_ref, wb_ref, o_ref):
    u = u_ref[...]
    a = _dot(u, wa_ref[...])
    b = _dot(u, wb_ref[...])
    o_ref[...] = (a * jax.nn.sigmoid(a) * b).astype(o_ref.dtype)


def _ffn_up(u, w_up, l, s):
    mrows, k = u.shape
    dff = w_up.shape[-1] // 2
    nb = dff // TN
    return pl.pallas_call(
        _swiglu_kernel,
        grid=(mrows // TM, nb),
        in_specs=[pl.BlockSpec((TM, k), lambda i, j: (i, 0)),
                  pl.BlockSpec((None, None, k, TN), lambda i, j: (l, s, 0, j)),
                  pl.BlockSpec((None, None, k, TN), lambda i, j: (l, s, 0, nb + j))],
        out_specs=pl.BlockSpec((TM, TN), lambda i, j: (i, j)),
        out_shape=jax.ShapeDtypeStruct((mrows, dff), BF16),
        compiler_params=_cparams(2),
        name="ffn_up",
    )(u, w_up, w_up)


def _resid_kernel(a_ref, w_ref, h_ref, gc_ref, gx_ref, o_ref, *, scale, ctx_len, slab):
    tm = a_ref.shape[0]
    acc = _dot(a_ref[...], w_ref[...])
    row0 = (pl.program_id(0) * tm) % slab
    rows = lax.broadcasted_iota(jnp.int32, (tm, 1), 0) + row0
    gate = jnp.where(rows < ctx_len, gc_ref[...], gx_ref[...])
    o_ref[...] = h_ref[...] + (scale * gate) * acc


def _resid_proj(a, w, widx, h, m, sub, scale, ctx_len, slab):
    mrows, k = a.shape
    n = w.shape[-1]
    g0 = (3 * sub + 2) * (n // TN)
    return pl.pallas_call(
        functools.partial(_resid_kernel, scale=scale, ctx_len=ctx_len, slab=slab),
        grid=(mrows // TM, n // TN),
        in_specs=[pl.BlockSpec((TM, k), lambda i, j: (i, 0)),
                  pl.BlockSpec((None, k, TN), lambda i, j: (widx, 0, j)),
                  pl.BlockSpec((TM, TN), lambda i, j: (i, j)),
                  pl.BlockSpec((None, 1, TN), lambda i, j: (0, 0, g0 + j)),
                  pl.BlockSpec((None, 1, TN), lambda i, j: (1 + (i * TM) // slab, 0, g0 + j))],
        out_specs=pl.BlockSpec((TM, TN), lambda i, j: (i, j)),
        out_shape=jax.ShapeDtypeStruct((mrows, n), F32),
        input_output_aliases={2: 0},
        compiler_params=_cparams(2),
        name="resid_proj",
    )(a, w, h, m, m)


def _branch_kernel(a_ref, b_ref, c_ref, pa_ref, pb_ref, pc_ref, ga_ref, gb_ref, gc_ref, o_ref):
    ya = _dot(a_ref[...], pa_ref[...])
    yb = _dot(b_ref[...], pb_ref[...])
    yc = _dot(c_ref[...], pc_ref[...])
    z = (jax.nn.sigmoid(ga_ref[...].astype(F32)) * ya
         + jax.nn.sigmoid(gb_ref[...].astype(F32)) * yb
         + jax.nn.sigmoid(gc_ref[...].astype(F32)) * yc)
    o_ref[...] = z.astype(o_ref.dtype)


def _branch_proj(a, b, c, pa, pb, pc, l, p):
    mrows = a.shape[0]
    d = pa.shape[-1]
    tn = TN
    g0 = P_GA // tn
    nb = d // tn
    return pl.pallas_call(
        _branch_kernel,
        grid=(mrows // TM, nb),
        in_specs=[pl.BlockSpec((TM, a.shape[1]), lambda i, j: (i, 0)),
                  pl.BlockSpec((TM, b.shape[1]), lambda i, j: (i, 0)),
                  pl.BlockSpec((TM, c.shape[1]), lambda i, j: (i, 0)),
                  pl.BlockSpec((None, pa.shape[1], tn), lambda i, j: (l, 0, j)),
                  pl.BlockSpec((None, pb.shape[1], tn), lambda i, j: (l, 0, j)),
                  pl.BlockSpec((None, pc.shape[1], tn), lambda i, j: (l, 0, j)),
                  pl.BlockSpec((TM, tn), lambda i, j: (i, g0 + j)),
                  pl.BlockSpec((TM, tn), lambda i, j: (i, g0 + nb + j)),
                  pl.BlockSpec((TM, tn), lambda i, j: (i, g0 + 2 * nb + j))],
        out_specs=pl.BlockSpec((TM, tn), lambda i, j: (i, j)),
        out_shape=jax.ShapeDtypeStruct((mrows, d), BF16),
        compiler_params=_cparams(2),
        name="branch_proj",
    )(a, b, c, pa, pb, pc, p, p, p)


def _conv_kernel(h_ref, bg_ref, cg_ref, w_ref, o_ref, *, ctx_len):
    n = h_ref.shape[0]
    x = cg_ref[...].astype(F32) * h_ref[...].astype(F32)
    row = lax.broadcasted_iota(jnp.int32, (n, 1), 0)
    first = (row == 0) | (row == ctx_len)
    last = (row == ctx_len - 1) | (row == n - 1)
    prev = jnp.where(first, 0.0, pltpu.roll(x, 1, axis=0))
    nxt = jnp.where(last, 0.0, pltpu.roll(x, n - 1, axis=0))
    w = w_ref[...]
    y = prev * w[0:1, :] + x * w[1:2, :] + nxt * w[2:3, :]
    o_ref[...] = (bg_ref[...].astype(F32) * y).astype(o_ref.dtype)


def _conv_branch(p, conv_w, bsz, ctx_len, slab):
    tc = 256
    nb = CONV_W // tc
    return pl.pallas_call(
        functools.partial(_conv_kernel, ctx_len=ctx_len),
        grid=(bsz, nb),
        in_specs=[pl.BlockSpec((slab, tc), lambda b, j: (b, P_H // tc + j)),
                  pl.BlockSpec((slab, tc), lambda b, j: (b, P_BG // tc + j)),
                  pl.BlockSpec((slab, tc), lambda b, j: (b, P_CG // tc + j)),
                  pl.BlockSpec((conv_w.shape[0], tc), lambda b, j: (0, j))],
        out_specs=pl.BlockSpec((slab, tc), lambda b, j: (b, j)),
        out_shape=jax.ShapeDtypeStruct((bsz * slab, CONV_W), BF16),
        compiler_params=_cparams(2),
        name="conv_branch",
    )(p, p, p, conv_w)


def _log_sigmoid(x):
    return jnp.minimum(x, 0.0) - jnp.log(1.0 + jnp.exp(-jnp.abs(x)))


def _gla_kernel(q_ref, k_ref, v_ref, g_ref, lr_ref, cos_ref, sin_ref, wdf_ref, wdb_ref,
                bdf_ref, bdb_ref, ng_ref, o_ref, qd_s, q2_s, ki_s, ke_s, dec_s, oacc_s, st_s,
                *, ctx_len):
    n = q_ref.shape[0]
    pair = 2 * GLA_CHUNK
    npair = n // pair
    cpair = ctx_len // pair
    half = GLA_DKP // 2
    cos = cos_ref[...]
    sin = sin_ref[...]

    def rope(x):
        x1 = x[:, :half]
        x2 = x[:, half:]
        return jnp.concatenate([x1 * cos - x2 * sin, x1 * sin + x2 * cos], axis=1)

    qr = rope(q_ref[...].astype(F32)) * (GLA_DK ** -0.5)
    kr = rope(k_ref[...].astype(F32))
    lr = lr_ref[...]
    ridx = lax.broadcasted_iota(jnp.int32, (n, 1), 0) % GLA_CHUNK
    pi = lax.broadcasted_iota(jnp.int32, (pair, pair), 0)
    pj = lax.broadcasted_iota(jnp.int32, (pair, pair), 1)
    same = (pi < GLA_CHUNK) == (pj < GLA_CHUNK)

    def spread(t4):
        return jnp.broadcast_to(t4, (npair, 2, GLA_CHUNK, GLA_DKP)).reshape(n, GLA_DKP)

    def chunk_sums(la):
        cum = la
        for sft in (1, 2, 4, 8, 16, 32):
            cum = cum + jnp.where(ridx >= sft, pltpu.roll(cum, sft, axis=0), 0.0)
        tot4 = cum.reshape(npair, 2, GLA_CHUNK, GLA_DKP)[:, :, GLA_CHUNK - 1:GLA_CHUNK, :]
        return cum, tot4

    def stage(cum, tot4, forward):
        lo, hi = tot4[:, 0:1], tot4[:, 1:2]
        zero = jnp.zeros_like(lo)
        if forward:
            q_add = spread(jnp.concatenate([zero, lo], axis=1))
            k_add = spread(jnp.concatenate([hi, zero], axis=1))
        else:
            q_add = spread(jnp.concatenate([hi, zero], axis=1))
            k_add = spread(jnp.concatenate([zero, lo], axis=1))
        qd_s[...] = (qr * jnp.exp(cum)).astype(BF16)
        q2_s[...] = (qr * jnp.exp(cum + q_add)).astype(BF16)
        ki_s[...] = (kr * jnp.exp(-cum)).astype(BF16)
        ke_s[...] = (kr * jnp.exp(spread(tot4) - cum + k_add)).astype(BF16)
        dec_s[...] = jnp.exp(lo[:, 0] + hi[:, 0])

    def scan(pair_of, steps, diag, off, first):
        def body(i, carry):
            c = pair_of(i)
            rows = pl.ds(pl.multiple_of(c * pair, pair), pair)
            q2 = q2_s[rows, :]
            ki = ki_s[rows, :]
            vv = v_ref[rows, :]
            att = jnp.where(diag, _dot_nt(qd_s[rows, :], ki), jnp.where(off, _dot_nt(q2, ki), 0.0))
            st = st_s[...]
            o = _dot(att.astype(BF16), vv) + _dot_nt(q2, st.astype(BF16))
            if first:
                oacc_s[rows, :] = o
            else:
                oacc_s[rows, :] += o
            st_s[...] = st * dec_s[c] + _dot_tn(vv, ke_s[rows, :])
            return carry
        lax.fori_loop(0, steps, body, 0, unroll=2)

    la = _log_sigmoid(_dot(lr, wdf_ref[...]) + bdf_ref[...]) / GLA_TAU
    cum, tot4 = chunk_sums(la)
    stage(cum, tot4, True)
    st_s[...] = jnp.zeros_like(st_s)
    scan(lambda i: i, npair, same & (pi >= pj), pi > pj, True)

    la = _log_sigmoid(_dot(lr, wdb_ref[...]) + bdb_ref[...]) / GLA_TAU
    cum, tot4 = chunk_sums(la)
    stage(spread(tot4) - cum + la, tot4, False)
    st_s[...] = jnp.zeros_like(st_s)
    scan(lambda i: cpair - 1 - i, cpair, same & (pi <= pj), pi < pj, False)
    scan(lambda i: npair - 1 - i, npair - cpair, same & (pi <= pj), pi < pj, False)

    o = oacc_s[...]
    o = o * lax.rsqrt(jnp.mean(o * o, axis=-1, keepdims=True) + EPS) * ng_ref[...]
    g = g_ref[...].astype(F32)
    o_ref[...] = (o * (g * jax.nn.sigmoid(g))).astype(o_ref.dtype)


def _gla_branch(p1, pq, cos, sin, wdf, wdb, bdf, bdb, ng, bsz, ctx_len, slab):
    npair = slab // (2 * GLA_CHUNK)
    head = lambda b, h: (h, 0, 0)
    return pl.pallas_call(
        functools.partial(_gla_kernel, ctx_len=ctx_len),
        grid=(bsz, GLA_HEADS),
        in_specs=[pl.BlockSpec((slab, GLA_DKP), lambda b, h: (b, PQ_Q // GLA_DKP + h)),
                  pl.BlockSpec((slab, GLA_DKP), lambda b, h: (b, PQ_K // GLA_DKP + h)),
                  pl.BlockSpec((slab, GLA_DV), lambda b, h: (b, P_V // GLA_DV + h)),
                  pl.BlockSpec((slab, GLA_DV), lambda b, h: (b, P_G // GLA_DV + h)),
                  pl.BlockSpec((slab, 128), lambda b, h: (b, PQ_Q // 128)),
                  pl.BlockSpec((slab, GLA_DKP // 2), lambda b, h: (0, 0)),
                  pl.BlockSpec((slab, GLA_DKP // 2), lambda b, h: (0, 0)),
                  pl.BlockSpec((None, 128, GLA_DKP), head),
                  pl.BlockSpec((None, 128, GLA_DKP), head),
                  pl.BlockSpec((None, 1, GLA_DKP), head),
                  pl.BlockSpec((None, 1, GLA_DKP), head),
                  pl.BlockSpec((None, 1, GLA_DV), head)],
        out_specs=pl.BlockSpec((slab, GLA_DV), lambda b, h: (b, h)),
        out_shape=jax.ShapeDtypeStruct((bsz * slab, GLA_HEADS * GLA_DV), BF16),
        scratch_shapes=[pltpu.VMEM((slab, GLA_DKP), BF16),
                        pltpu.VMEM((slab, GLA_DKP), BF16),
                        pltpu.VMEM((slab, GLA_DKP), BF16),
                        pltpu.VMEM((slab, GLA_DKP), BF16),
                        pltpu.VMEM((npair, 1, GLA_DKP), F32),
                        pltpu.VMEM((slab, GLA_DV), F32),
                        pltpu.VMEM((GLA_DV, GLA_DKP), F32)],
        compiler_params=_cparams(2),
        name="gla_branch",
    )(pq, pq, p1, p1, pq, cos, sin, wdf, wdb, bdf, bdb, ng)


def _na_group_base(g, rows):
    return min(max(g * NA_GROUP - NA_KR // 2, 0), rows - NA_WIN)


def _na_build_bias(rpb_ref, bias_ref, rows):
    w = GRID_W
    qc = lax.broadcasted_iota(jnp.int32, (w, w), 0)
    kc = lax.broadcasted_iota(jnp.int32, (w, w), 1)
    win = jnp.clip(qc - NA_KC // 2, 0, w - NA_KC)
    col_ok = (kc >= win) & (kc < win + NA_KC)
    neg = jnp.full((w, w), -jnp.inf, F32)
    blocks = []
    for ro in range(2 * NA_KR - 1):
        row = jnp.broadcast_to(rpb_ref[ro:ro + 1, :], (w, 128))
        band = pltpu.roll(row, 128 - (NA_KC - 1), axis=1, stride=1, stride_axis=0)
        blocks.append(jnp.where(col_ok, band[:, :w] * LOG2E, neg))
    ngroups = rows // NA_GROUP
    for kind, g in enumerate((0, 1, ngroups - 1)):
        base = _na_group_base(g, rows)
        for qr in range(NA_GROUP):
            r_abs = g * NA_GROUP + qr
            win_r = min(max(r_abs - NA_KR // 2, 0), rows - NA_KR)
            pieces = []
            for kr in range(NA_WIN):
                key_r = base + kr
                inside = win_r <= key_r < win_r + NA_KR
                pieces.append(blocks[key_r - r_abs + NA_KR - 1] if inside else neg)
            bias_ref[kind, qr * w:(qr + 1) * w, :] = jnp.concatenate(pieces, axis=1)


def _na_kernel(q_ref, k_ref, v_ref, rpb_ref, o_ref, bias_ref, *, ctx_len, rows):
    scale = NA_HD ** -0.5 * LOG2E
    gq = NA_GROUP * GRID_W
    nwin = NA_WIN * GRID_W
    kc = k_ref[0:ctx_len, :]
    vc = v_ref[0:ctx_len, :]

    @pl.when(pl.program_id(1) == 0)
    def _():
        _na_build_bias(rpb_ref, bias_ref, rows)

    s = _dot_nt(q_ref[0:ctx_len, :], kc) * scale
    e = jnp.exp2(s - jnp.max(s, axis=-1, keepdims=True))
    pr = e / jnp.sum(e, axis=-1, keepdims=True)
    o_ref[0:ctx_len, :] = _dot(pr.astype(BF16), vc).astype(o_ref.dtype)

    ngroups = rows // NA_GROUP
    for g in range(ngroups):
        kind = 0 if g == 0 else (2 if g == ngroups - 1 else 1)
        q0 = ctx_len + g * gq
        k0 = ctx_len + _na_group_base(g, rows) * GRID_W
        qg = q_ref[q0:q0 + gq, :]
        s_loc = _dot_nt(qg, k_ref[k0:k0 + nwin, :]) * scale + bias_ref[kind]
        s_ctx = _dot_nt(qg, kc) * scale
        m = jnp.maximum(jnp.max(s_loc, axis=-1, keepdims=True), jnp.max(s_ctx, axis=-1, keepdims=True))
        e_loc = jnp.exp2(s_loc - m)
        e_ctx = jnp.exp2(s_ctx - m)
        den = jnp.sum(e_loc, axis=-1, keepdims=True) + jnp.sum(e_ctx, axis=-1, keepdims=True)
        o = _dot(e_loc.astype(BF16), v_ref[k0:k0 + nwin, :]) + _dot(e_ctx.astype(BF16), vc)
        o_ref[q0:q0 + gq, :] = (o / den).astype(o_ref.dtype)


def _na_branch(p, rpb, l, bsz, ctx_len, slab):
    rows = (slab - ctx_len) // GRID_W
    return pl.pallas_call(
        functools.partial(_na_kernel, ctx_len=ctx_len, rows=rows),
        grid=(NA_HEADS, bsz),
        in_specs=[pl.BlockSpec((slab, NA_HD), lambda h, b: (b, P_NQ // NA_HD + h)),
                  pl.BlockSpec((slab, NA_HD), lambda h, b: (b, P_NK // NA_HD + h)),
                  pl.BlockSpec((slab, NA_HD), lambda h, b: (b, P_NV // NA_HD + h)),
                  pl.BlockSpec((None, None) + rpb.shape[2:], lambda h, b: (l, h, 0, 0))],
        out_specs=pl.BlockSpec((slab, NA_HD), lambda h, b: (b, h)),
        out_shape=jax.ShapeDtypeStruct((bsz * slab, NA_W), BF16),
        scratch_shapes=[pltpu.VMEM((3, NA_GROUP * GRID_W, NA_WIN * GRID_W), F32)],
        compiler_params=_cparams(2),
        name="na_branch",
    )(p, p, p, rpb)


def _gla_dk_order():
    q = GLA_DK // 4
    one = np.full((GLA_DKP,), -1, np.int64)
    one[0:q] = np.arange(0, q)
    one[q:2 * q] = np.arange(2 * q, 3 * q)
    one[GLA_DKP // 2:GLA_DKP // 2 + q] = np.arange(q, 2 * q)
    one[GLA_DKP // 2 + q:GLA_DKP // 2 + 2 * q] = np.arange(3 * q, 4 * q)
    out = np.concatenate([np.where(one >= 0, one + h * GLA_DK, -1) for h in range(GLA_HEADS)])
    return out


def _take_cols(w, cols, axis=-1):
    axis = axis % w.ndim
    pieces = []
    i = 0
    n = len(cols)
    while i < n:
        j = i + 1
        if cols[i] < 0:
            while j < n and cols[j] < 0:
                j += 1
            pieces.append(jnp.zeros(w.shape[:axis] + (j - i,) + w.shape[axis + 1:], w.dtype))
        else:
            while j < n and cols[j] == cols[j - 1] + 1:
                j += 1
            pieces.append(lax.slice_in_dim(w, int(cols[i]), int(cols[i]) + (j - i), axis=axis))
        i = j
    return jnp.concatenate(pieces, axis=axis)


def _in_proj_offsets():
    o_q = 3 * CONV_W
    o_k = o_q + GLA_HEADS * GLA_DK
    o_v = o_k + GLA_HEADS * GLA_DK
    o_lr = o_v + 2 * GLA_HEADS * GLA_DV
    o_nq = o_lr + 2 * GLA_RANK
    return o_q, o_k, o_v, o_lr, o_nq


def _qk_weights(w_in_t):
    o_q, o_k, o_v, o_lr, o_nq = _in_proj_offsets()
    order = _gla_dk_order()
    q_order = np.where(order >= 0, order + o_q, -1)
    q_order[LR_LANE:LR_LANE + 2 * GLA_RANK] = o_lr + np.arange(2 * GLA_RANK)
    k_order = np.where(order >= 0, order + o_k, -1)
    return _take_cols(w_in_t, np.concatenate([q_order, k_order]), axis=1)


def _rope_tables(ctx_len, seq):
    q = GLA_DK // 4
    pos = jnp.arange(seq)
    freq = ROPE_THETA ** (-jnp.arange(q, dtype=F32) / q)
    ang_r = (pos // GRID_W).astype(F32)[:, None] * freq[None, :]
    ang_c = (pos % GRID_W).astype(F32)[:, None] * freq[None, :]
    pad = GLA_DKP // 2 - 2 * q
    cos = jnp.concatenate([jnp.cos(ang_r), jnp.cos(ang_c), jnp.ones((seq, pad), F32)], axis=1)
    sin = jnp.concatenate([jnp.sin(ang_r), jnp.sin(ang_c), jnp.zeros((seq, pad), F32)], axis=1)
    cos = jnp.concatenate([jnp.ones((ctx_len, GLA_DKP // 2), F32), cos], axis=0)
    sin = jnp.concatenate([jnp.zeros((ctx_len, GLA_DKP // 2), F32), sin], axis=0)
    return cos, sin


def kernel(x, c, ctx, c_ctx, mod_a, mod_b, mod_bias, norm_g, ffn_up, ffn_down, w_in, conv_w,
           gla_decay_w, gla_decay_b, gla_norm_g, na_rpb, w_branch_conv, w_branch_gla, w_branch_na,
           w_out, final_g):
    bsz, seq, d = x.shape
    ctx_len = ctx.shape[1]
    depth = mod_a.shape[0]
    slab = ctx_len + seq
    rows = seq // GRID_W
    assert slab % TM == 0 and d % TN == 0 and seq % ctx_len == 0
    assert rows % NA_GROUP == 0 and rows >= NA_WIN
    assert ctx_len % (2 * GLA_CHUNK) == 0 and seq % (2 * GLA_CHUNK) == 0

    mrows = 16
    cvec = jnp.concatenate([c_ctx[None, :], c, jnp.zeros((mrows - 1 - bsz, d), F32)], axis=0)
    mods = _modulation(cvec, mod_a, mod_b, mod_bias)[:, :1 + bsz].reshape(depth, 1 + bsz, 1, N_MOD * d)

    o_q, _, o_v, o_lr, o_nq = _in_proj_offsets()
    p_runs = [(0, o_q), (o_v, o_lr - o_v), (o_nq, w_in.shape[-1] - o_nq)]
    w_in_t = jnp.swapaxes(w_in, 1, 2)
    wq_t = _qk_weights(w_in_t)
    ffn_up_b = ffn_up.astype(BF16)
    ffn_down_b = ffn_down.astype(BF16).reshape(depth * 2, ffn_down.shape[2], d)
    wb_conv = w_branch_conv.astype(BF16)
    wb_gla = w_branch_gla.astype(BF16)
    wb_na = w_branch_na.astype(BF16)
    w_out_b = w_out.astype(BF16)
    order = _gla_dk_order()
    dw = _take_cols(gla_decay_w, order).reshape(depth, 2, GLA_RANK, GLA_HEADS, GLA_DKP)
    dw = dw.transpose(0, 1, 3, 2, 4)
    wdf = jnp.zeros((depth, GLA_HEADS, 128, GLA_DKP), F32).at[:, :, LR_LANE:LR_LANE + GLA_RANK].set(dw[:, 0])
    wdb = jnp.zeros((depth, GLA_HEADS, 128, GLA_DKP), F32).at[:, :, LR_LANE + GLA_RANK:LR_LANE + 2 * GLA_RANK].set(dw[:, 1])
    wdf = wdf.astype(BF16)
    wdb = wdb.astype(BF16)
    db = _take_cols(gla_decay_b, order).reshape(depth, 2, GLA_HEADS, 1, GLA_DKP)
    ng = gla_norm_g.reshape(depth, GLA_HEADS, 1, GLA_DV)
    cos, sin = _rope_tables(ctx_len, seq)
    rpb_pad = jnp.pad(na_rpb, ((0, 0), (0, 0), (0, 1), (0, 128 - na_rpb.shape[-1])))

    h = jnp.concatenate([ctx, x], axis=1).reshape(bsz * slab, d)
    for l in range(depth):
        m = mods[l]
        nm = functools.partial(_norm_mod, m=m, ctx_len=ctx_len, slab=slab)
        rp = functools.partial(_resid_proj, m=m, ctx_len=ctx_len, slab=slab)
        act = _ffn_up(nm(h, norm_g[l, 0:1], sub=0), ffn_up_b, l, 0)
        h = rp(act, ffn_down_b, 2 * l, h, sub=0, scale=0.5)
        u = nm(h, norm_g[l, 1:2], sub=1)
        p = _proj_runs(u, w_in_t, l, p_runs)
        pq = _proj_runs(u, wq_t, l, [(0, wq_t.shape[1])])
        ya = _conv_branch(p, conv_w[l], bsz, ctx_len, slab)
        yb = _gla_branch(p, pq, cos, sin, wdf[l], wdb[l], db[l, 0], db[l, 1], ng[l], bsz, ctx_len, slab)
        yc = _na_branch(p, rpb_pad, l, bsz, ctx_len, slab)
        z = _branch_proj(ya, yb, yc, wb_conv, wb_gla, wb_na, l, p)
        h = rp(z, w_out_b, l, h, sub=1, scale=1.0)
        act = _ffn_up(nm(h, norm_g[l, 2:3], sub=2), ffn_up_b, l, 1)
        h = rp(act, ffn_down_b, 2 * l + 1, h, sub=2, scale=0.5)
    return _final_norm(h, final_g.reshape(1, d), bsz, ctx_len, seq)
```

```python
import functools

import numpy as np
import jax
import jax.numpy as jnp
from jax import lax
from jax.experimental import pallas as pl
from jax.experimental.pallas import tpu as pltpu

F32 = jnp.float32
BF16 = jnp.bfloat16

GRID_W = 64
N_MOD = 9
CONV_W = 1024
GLA_HEADS = 4
GLA_DK = 192
GLA_DV = 384
GLA_DKP = 256
GLA_RANK = 16
GLA_TAU = 16.0
GLA_CHUNK = 64
NA_HEADS = 12
NA_HD = 128
NA_W = NA_HEADS * NA_HD
NA_KR = 8
NA_KC = 16
NA_GROUP = 4
NA_WIN = NA_KR + NA_GROUP - 1
ROPE_THETA = 10000.0
EPS = 1e-6
LOG2E = 1.4426950408889634

P_H, P_BG, P_CG = 0, 1024, 2048
P_V, P_G = 3072, 4608
P_NQ, P_NK, P_NV = 6144, 7680, 9216
P_GA = 10752
PQ_Q, PQ_K = 0, 1024
LR_LANE = 96

VMEM_LIMIT = 56 * 1024 * 1024
TM = 1152
TM_WIDE = 2 * TM
TN = 512
NORM_ROWS = 256
NORM_STEP_ROWS = 16


def _cparams(n_axes):
    return pltpu.CompilerParams(dimension_semantics=("arbitrary",) * n_axes,
                                vmem_limit_bytes=VMEM_LIMIT)


def _dot(a, b):
    return jnp.dot(a, b, preferred_element_type=F32)


def _dot_nt(a, b):
    return lax.dot_general(a, b, (((1,), (1,)), ((), ())), preferred_element_type=F32)


def _dot_tn(a, b):
    return lax.dot_general(a, b, (((0,), (0,)), ((), ())), preferred_element_type=F32)


def _sigmoid(x):
    return 0.5 * jnp.tanh(0.5 * x) + 0.5


def _mod_a_kernel(c_ref, a_ref, o_ref):
    c = c_ref[...]
    s = (c * jax.nn.sigmoid(c)).astype(BF16)
    o_ref[...] = _dot(s, a_ref[...].astype(BF16))


def _mod_b_kernel(t_ref, b_ref, bias_ref, o_ref):
    o_ref[...] = _dot(t_ref[...].astype(BF16), b_ref[...].astype(BF16)) + bias_ref[...]


def _modulation(cvec, mod_a, mod_b, mod_bias):
    depth, d, rank = mod_a.shape
    rows = cvec.shape[0]
    nm = mod_b.shape[-1]
    tn = min(d, 2048)
    assert nm % tn == 0
    t = pl.pallas_call(
        _mod_a_kernel,
        grid=(depth,),
        in_specs=[pl.BlockSpec((rows, d), lambda l: (0, 0)),
                  pl.BlockSpec((None, d, rank), lambda l: (l, 0, 0))],
        out_specs=pl.BlockSpec((None, rows, rank), lambda l: (l, 0, 0)),
        out_shape=jax.ShapeDtypeStruct((depth, rows, rank), F32),
        compiler_params=_cparams(1),
        name="mod_a",
    )(cvec, mod_a)
    return pl.pallas_call(
        _mod_b_kernel,
        grid=(depth, nm // tn),
        in_specs=[pl.BlockSpec((None, rows, rank), lambda l, j: (l, 0, 0)),
                  pl.BlockSpec((None, rank, tn), lambda l, j: (l, 0, j)),
                  pl.BlockSpec((None, 1, tn), lambda l, j: (l, 0, j))],
        out_specs=pl.BlockSpec((None, rows, tn), lambda l, j: (l, 0, j)),
        out_shape=jax.ShapeDtypeStruct((depth, rows, nm), F32),
        compiler_params=_cparams(2),
        name="mod_b",
    )(t, mod_b, mod_bias.reshape(depth, 1, nm))


def _norm_mod_kernel(h_ref, g_ref, shc_ref, scc_ref, shx_ref, scx_ref, o_ref, *, ctx_len, slab):
    tr = h_ref.shape[0]
    is_ctx = (pl.program_id(0) * tr) % slab < ctx_len
    sc = jnp.where(is_ctx, scc_ref[...], scx_ref[...])
    sh = jnp.where(is_ctx, shc_ref[...], shx_ref[...])
    gs = g_ref[...] * (1.0 + sc)
    inv_d = 1.0 / h_ref.shape[1]

    def body(i, carry):
        rows = pl.ds(pl.multiple_of(i * NORM_STEP_ROWS, NORM_STEP_ROWS), NORM_STEP_ROWS)
        x = h_ref[rows, :]
        rstd = lax.rsqrt(jnp.sum(x * x, axis=-1, keepdims=True) * inv_d + EPS)
        o_ref[rows, :] = ((h_ref[rows, :] * rstd) * gs + sh).astype(o_ref.dtype)
        return carry

    lax.fori_loop(0, tr // NORM_STEP_ROWS, body, 0, unroll=True)


def _norm_mod(h, g, m, sub, ctx_len, slab):
    mrows, d = h.shape
    tr = NORM_ROWS
    assert slab % tr == 0 and ctx_len % tr == 0
    ctx_row = lambda c: (lambda j: (0, 0, c))
    batch_row = lambda c: (lambda j: (1 + (j * tr) // slab, 0, c))
    return pl.pallas_call(
        functools.partial(_norm_mod_kernel, ctx_len=ctx_len, slab=slab),
        grid=(mrows // tr,),
        in_specs=[pl.BlockSpec((tr, d), lambda j: (j, 0)),
                  pl.BlockSpec((1, d), lambda j: (0, 0)),
                  pl.BlockSpec((None, 1, d), ctx_row(3 * sub)),
                  pl.BlockSpec((None, 1, d), ctx_row(3 * sub + 1)),
                  pl.BlockSpec((None, 1, d), batch_row(3 * sub)),
                  pl.BlockSpec((None, 1, d), batch_row(3 * sub + 1))],
        out_specs=pl.BlockSpec((tr, d), lambda j: (j, 0)),
        out_shape=jax.ShapeDtypeStruct((mrows, d), BF16),
        compiler_params=_cparams(1),
        name="norm_mod",
    )(h, g, m, m, m, m)


def _final_norm_kernel(h_ref, g_ref, o_ref):
    x = h_ref[...]
    o_ref[...] = x * lax.rsqrt(jnp.mean(x * x, axis=-1, keepdims=True) + EPS) * g_ref[...]


def _final_norm(h, g, bsz, ctx_len, seq):
    d = h.shape[-1]
    slab = ctx_len + seq
    tr = ctx_len
    return pl.pallas_call(
        _final_norm_kernel,
        grid=(bsz, seq // tr),
        in_specs=[pl.BlockSpec((None, tr, d), lambda b, j: (b, 1 + j, 0)),
                  pl.BlockSpec((1, d), lambda b, j: (0, 0))],
        out_specs=pl.BlockSpec((None, tr, d), lambda b, j: (b, j, 0)),
        out_shape=jax.ShapeDtypeStruct((bsz, seq, d), F32),
        compiler_params=_cparams(2),
        name="final_norm",
    )(h.reshape(bsz, slab, d), g)


def _proj_t_kernel(a_ref, wt_ref, o_ref):
    o_ref[...] = _dot_nt(a_ref[...], wt_ref[0].astype(BF16)).astype(o_ref.dtype)


def _proj_runs(a, wt, l, runs):
    mrows, k = a.shape
    assert all(s % 8 == 0 and n % TN == 0 for s, n in runs)
    starts = np.concatenate([s + TN * np.arange(n // TN) for s, n in runs])
    first = int(starts[0])
    step_at = [int(i) for i in np.nonzero(np.diff(starts) != TN)[0] + 1]
    jumps = [int(starts[i] - starts[i - 1] - TN) for i in step_at]

    def wrow(j):
        r = j * TN + first
        for at, jump in zip(step_at, jumps):
            r = r + jnp.where(j >= at, jump, 0)
        return pl.multiple_of(r, 8)

    wide = len(starts) >= 16
    tm = TM_WIDE if wide else TM
    a_spec = (pl.BlockSpec((tm, k), lambda i, j: (i, 0), pipeline_mode=pl.Buffered(1)) if wide
              else pl.BlockSpec((tm, k), lambda i, j: (i, 0)))
    return pl.pallas_call(
        _proj_t_kernel,
        grid=(mrows // tm, len(starts)),
        in_specs=[a_spec,
                  pl.BlockSpec((pl.Element(1), pl.Element(TN), pl.Element(k)), lambda i, j: (l, wrow(j), 0))],
        out_specs=pl.BlockSpec((tm, TN), lambda i, j: (i, j)),
        out_shape=jax.ShapeDtypeStruct((mrows, len(starts) * TN), BF16),
        compiler_params=_cparams(2),
        name="in_proj_runs",
    )(a, wt)


def _swiglu_kernel(u_ref, wa_ref, wb_ref, o_ref):
    u = u_ref[...]
    a = _dot(u, wa_ref[...])
    b = _dot(u, wb_ref[...])
    o_ref[...] = (a * _sigmoid(a) * b).astype(o_ref.dtype)


def _ffn_up(u, w_up, l, s):
    mrows, k = u.shape
    dff = w_up.shape[-1] // 2
    nb = dff // TN
    return pl.pallas_call(
        _swiglu_kernel,
        grid=(mrows // TM, nb),
        in_specs=[pl.BlockSpec((TM, k), lambda i, j: (i, 0)),
                  pl.BlockSpec((None, None, k, TN), lambda i, j: (l, s, 0, j)),
                  pl.BlockSpec((None, None, k, TN), lambda i, j: (l, s, 0, nb + j))],
        out_specs=pl.BlockSpec((TM, TN), lambda i, j: (i, j)),
        out_shape=jax.ShapeDtypeStruct((mrows, dff), BF16),
        compiler_params=_cparams(2),
        name="ffn_up",
    )(u, w_up, w_up)


def _resid_kernel(a_ref, w_ref, h_ref, gc_ref, gx_ref, o_ref, *, scale, ctx_len, slab):
    tm = a_ref.shape[0]
    acc = _dot(a_ref[...], w_ref[...])
    row0 = (pl.program_id(0) * tm) % slab
    rows = lax.broadcasted_iota(jnp.int32, (tm, 1), 0) + row0
    gate = jnp.where(rows < ctx_len, gc_ref[...], gx_ref[...])
    o_ref[...] = h_ref[...] + (scale * gate) * acc


def _resid_proj(a, w, widx, h, m, sub, scale, ctx_len, slab):
    mrows, k = a.shape
    n = w.shape[-1]
    g0 = (3 * sub + 2) * (n // TN)
    return pl.pallas_call(
        functools.partial(_resid_kernel, scale=scale, ctx_len=ctx_len, slab=slab),
        grid=(mrows // TM, n // TN),
        in_specs=[pl.BlockSpec((TM, k), lambda i, j: (i, 0)),
                  pl.BlockSpec((None, k, TN), lambda i, j: (widx, 0, j)),
                  pl.BlockSpec((TM, TN), lambda i, j: (i, j)),
                  pl.BlockSpec((None, 1, TN), lambda i, j: (0, 0, g0 + j)),
                  pl.BlockSpec((None, 1, TN), lambda i, j: (1 + (i * TM) // slab, 0, g0 + j))],
        out_specs=pl.BlockSpec((TM, TN), lambda i, j: (i, j)),
        out_shape=jax.ShapeDtypeStruct((mrows, n), F32),
        input_output_aliases={2: 0},
        compiler_params=_cparams(2),
        name="resid_proj",
    )(a, w, h, m, m)


def _branch_kernel(a_ref, b_ref, c_ref, pa_ref, pb_ref, pc_ref, ga_ref, gb_ref, gc_ref, o_ref):
    ya = _dot(a_ref[...], pa_ref[...])
    yb = _dot(b_ref[...], pb_ref[...])
    yc = _dot(c_ref[...], pc_ref[...])
    z = (_sigmoid(ga_ref[...].astype(F32)) * ya
         + _sigmoid(gb_ref[...].astype(F32)) * yb
         + _sigmoid(gc_ref[...].astype(F32)) * yc)
    o_ref[...] = z.astype(o_ref.dtype)


def _branch_proj(a, b, c, pa, pb, pc, l, p):
    mrows = a.shape[0]
    d = pa.shape[-1]
    tn = TN
    g0 = P_GA // tn
    nb = d // tn
    return pl.pallas_call(
        _branch_kernel,
        grid=(mrows // TM, nb),
        in_specs=[pl.BlockSpec((TM, a.shape[1]), lambda i, j: (i, 0)),
                  pl.BlockSpec((TM, b.shape[1]), lambda i, j: (i, 0)),
                  pl.BlockSpec((TM, c.shape[1]), lambda i, j: (i, 0)),
                  pl.BlockSpec((None, pa.shape[1], tn), lambda i, j: (l, 0, j)),
                  pl.BlockSpec((None, pb.shape[1], tn), lambda i, j: (l, 0, j)),
                  pl.BlockSpec((None, pc.shape[1], tn), lambda i, j: (l, 0, j)),
                  pl.BlockSpec((TM, tn), lambda i, j: (i, g0 + j)),
                  pl.BlockSpec((TM, tn), lambda i, j: (i, g0 + nb + j)),
                  pl.BlockSpec((TM, tn), lambda i, j: (i, g0 + 2 * nb + j))],
        out_specs=pl.BlockSpec((TM, tn), lambda i, j: (i, j)),
        out_shape=jax.ShapeDtypeStruct((mrows, d), BF16),
        compiler_params=_cparams(2),
        name="branch_proj",
    )(a, b, c, pa, pb, pc, p, p, p)


def _conv_kernel(h_ref, bg_ref, cg_ref, w_ref, o_ref, *, ctx_len):
    n = h_ref.shape[0]
    x = cg_ref[...].astype(F32) * h_ref[...].astype(F32)
    row = lax.broadcasted_iota(jnp.int32, (n, 1), 0)
    first = (row == 0) | (row == ctx_len)
    last = (row == ctx_len - 1) | (row == n - 1)
    prev = jnp.where(first, 0.0, pltpu.roll(x, 1, axis=0))
    nxt = jnp.where(last, 0.0, pltpu.roll(x, n - 1, axis=0))
    w = w_ref[...]
    y = prev * w[0:1, :] + x * w[1:2, :] + nxt * w[2:3, :]
    o_ref[...] = (bg_ref[...].astype(F32) * y).astype(o_ref.dtype)


def _conv_branch(p, conv_w, bsz, ctx_len, slab):
    tc = 256
    nb = CONV_W // tc
    return pl.pallas_call(
        functools.partial(_conv_kernel, ctx_len=ctx_len),
        grid=(bsz, nb),
        in_specs=[pl.BlockSpec((slab, tc), lambda b, j: (b, P_H // tc + j)),
                  pl.BlockSpec((slab, tc), lambda b, j: (b, P_BG // tc + j)),
                  pl.BlockSpec((slab, tc), lambda b, j: (b, P_CG // tc + j)),
                  pl.BlockSpec((conv_w.shape[0], tc), lambda b, j: (0, j))],
        out_specs=pl.BlockSpec((slab, tc), lambda b, j: (b, j)),
        out_shape=jax.ShapeDtypeStruct((bsz * slab, CONV_W), BF16),
        compiler_params=_cparams(2),
        name="conv_branch",
    )(p, p, p, conv_w)


def _log_sigmoid(x):
    return jnp.minimum(x, 0.0) - jnp.log(1.0 + jnp.exp(-jnp.abs(x)))


def _gla_kernel(q_ref, k_ref, v_ref, g_ref, lr_ref, cos_ref, sin_ref, wdf_ref, wdb_ref,
                bdf_ref, bdb_ref, ng_ref, o_ref, qd_s, q2_s, ki_s, ke_s, dec_s, oacc_s, st_s,
                *, ctx_len):
    n = q_ref.shape[0]
    pair = 2 * GLA_CHUNK
    npair = n // pair
    cpair = ctx_len // pair
    half = GLA_DKP // 2
    cos = cos_ref[...]
    sin = sin_ref[...]

    def rope(x):
        x1 = x[:, :half]
        x2 = x[:, half:]
        return jnp.concatenate([x1 * cos - x2 * sin, x1 * sin + x2 * cos], axis=1)

    qr = rope(q_ref[...].astype(F32)) * (GLA_DK ** -0.5)
    kr = rope(k_ref[...].astype(F32))
    lr = lr_ref[...]
    ridx = lax.broadcasted_iota(jnp.int32, (n, 1), 0) % GLA_CHUNK
    pi = lax.broadcasted_iota(jnp.int32, (pair, pair), 0)
    pj = lax.broadcasted_iota(jnp.int32, (pair, pair), 1)
    same = (pi < GLA_CHUNK) == (pj < GLA_CHUNK)

    def spread(t4):
        return jnp.broadcast_to(t4, (npair, 2, GLA_CHUNK, GLA_DKP)).reshape(n, GLA_DKP)

    def chunk_sums(la):
        cum = la
        for sft in (1, 2, 4, 8, 16, 32):
            cum = cum + jnp.where(ridx >= sft, pltpu.roll(cum, sft, axis=0), 0.0)
        tot4 = cum.reshape(npair, 2, GLA_CHUNK, GLA_DKP)[:, :, GLA_CHUNK - 1:GLA_CHUNK, :]
        return cum, tot4

    def stage(cum, tot4, forward):
        lo, hi = tot4[:, 0:1], tot4[:, 1:2]
        zero = jnp.zeros_like(lo)
        if forward:
            q_add = spread(jnp.concatenate([zero, lo], axis=1))
            k_add = spread(jnp.concatenate([hi, zero], axis=1))
        else:
            q_add = spread(jnp.concatenate([hi, zero], axis=1))
            k_add = spread(jnp.concatenate([zero, lo], axis=1))
        qd_s[...] = (qr * jnp.exp(cum)).astype(BF16)
        q2_s[...] = (qr * jnp.exp(cum + q_add)).astype(BF16)
        ki_s[...] = (kr * jnp.exp(-cum)).astype(BF16)
        ke_s[...] = (kr * jnp.exp(spread(tot4) - cum + k_add)).astype(BF16)
        dec_s[...] = jnp.exp(lo[:, 0] + hi[:, 0])

    def scan(pair_of, steps, diag, off, first):
        def body(i, carry):
            c = pair_of(i)
            rows = pl.ds(pl.multiple_of(c * pair, pair), pair)
            q2 = q2_s[rows, :]
            ki = ki_s[rows, :]
            vv = v_ref[rows, :]
            att = jnp.where(diag, _dot_nt(qd_s[rows, :], ki), jnp.where(off, _dot_nt(q2, ki), 0.0))
            st = st_s[...]
            o = _dot(att.astype(BF16), vv) + _dot_nt(q2, st.astype(BF16))
            if first:
                oacc_s[rows, :] = o
            else:
                oacc_s[rows, :] += o
            st_s[...] = st * dec_s[c] + _dot_tn(vv, ke_s[rows, :])
            return carry
        lax.fori_loop(0, steps, body, 0, unroll=2)

    la = _log_sigmoid(_dot(lr, wdf_ref[...]) + bdf_ref[...]) / GLA_TAU
    cum, tot4 = chunk_sums(la)
    stage(cum, tot4, True)
    st_s[...] = jnp.zeros_like(st_s)
    scan(lambda i: i, npair, same & (pi >= pj), pi > pj, True)

    la = _log_sigmoid(_dot(lr, wdb_ref[...]) + bdb_ref[...]) / GLA_TAU
    cum, tot4 = chunk_sums(la)
    stage(spread(tot4) - cum + la, tot4, False)
    st_s[...] = jnp.zeros_like(st_s)
    scan(lambda i: cpair - 1 - i, cpair, same & (pi <= pj), pi < pj, False)
    scan(lambda i: npair - 1 - i, npair - cpair, same & (pi <= pj), pi < pj, False)

    o = oacc_s[...]
    o = o * lax.rsqrt(jnp.mean(o * o, axis=-1, keepdims=True) + EPS) * ng_ref[...]
    g = g_ref[...].astype(F32)
    o_ref[...] = (o * (g * jax.nn.sigmoid(g))).astype(o_ref.dtype)


def _gla_branch(p1, pq, cos, sin, wdf, wdb, bdf, bdb, ng, bsz, ctx_len, slab):
    npair = slab // (2 * GLA_CHUNK)
    head = lambda b, h: (h, 0, 0)
    return pl.pallas_call(
        functools.partial(_gla_kernel, ctx_len=ctx_len),
        grid=(bsz, GLA_HEADS),
        in_specs=[pl.BlockSpec((slab, GLA_DKP), lambda b, h: (b, PQ_Q // GLA_DKP + h)),
                  pl.BlockSpec((slab, GLA_DKP), lambda b, h: (b, PQ_K // GLA_DKP + h)),
                  pl.BlockSpec((slab, GLA_DV), lambda b, h: (b, P_V // GLA_DV + h)),
                  pl.BlockSpec((slab, GLA_DV), lambda b, h: (b, P_G // GLA_DV + h)),
                  pl.BlockSpec((slab, 128), lambda b, h: (b, PQ_Q // 128)),
                  pl.BlockSpec((slab, GLA_DKP // 2), lambda b, h: (0, 0)),
                  pl.BlockSpec((slab, GLA_DKP // 2), lambda b, h: (0, 0)),
                  pl.BlockSpec((None, 128, GLA_DKP), head),
                  pl.BlockSpec((None, 128, GLA_DKP), head),
                  pl.BlockSpec((None, 1, GLA_DKP), head),
                  pl.BlockSpec((None, 1, GLA_DKP), head),
                  pl.BlockSpec((None, 1, GLA_DV), head)],
        out_specs=pl.BlockSpec((slab, GLA_DV), lambda b, h: (b, h)),
        out_shape=jax.ShapeDtypeStruct((bsz * slab, GLA_HEADS * GLA_DV), BF16),
        scratch_shapes=[pltpu.VMEM((slab, GLA_DKP), BF16),
                        pltpu.VMEM((slab, GLA_DKP), BF16),
                        pltpu.VMEM((slab, GLA_DKP), BF16),
                        pltpu.VMEM((slab, GLA_DKP), BF16),
                        pltpu.VMEM((npair, 1, GLA_DKP), F32),
                        pltpu.VMEM((slab, GLA_DV), F32),
                        pltpu.VMEM((GLA_DV, GLA_DKP), F32)],
        compiler_params=_cparams(2),
        name="gla_branch",
    )(pq, pq, p1, p1, pq, cos, sin, wdf, wdb, bdf, bdb, ng)


def _na_group_base(g, rows):
    return min(max(g * NA_GROUP - NA_KR // 2, 0), rows - NA_WIN)


def _na_build_bias(rpb_ref, bias_ref, rows):
    w = GRID_W
    qc = lax.broadcasted_iota(jnp.int32, (w, w), 0)
    kc = lax.broadcasted_iota(jnp.int32, (w, w), 1)
    win = jnp.clip(qc - NA_KC // 2, 0, w - NA_KC)
    col_ok = (kc >= win) & (kc < win + NA_KC)
    neg = jnp.full((w, w), -jnp.inf, F32)
    blocks = []
    for ro in range(2 * NA_KR - 1):
        row = jnp.broadcast_to(rpb_ref[ro:ro + 1, :], (w, 128))
        band = pltpu.roll(row, 128 - (NA_KC - 1), axis=1, stride=1, stride_axis=0)
        blocks.append(jnp.where(col_ok, band[:, :w] * LOG2E, neg))
    ngroups = rows // NA_GROUP
    for kind, g in enumerate((0, 1, ngroups - 1)):
        base = _na_group_base(g, rows)
        for qr in range(NA_GROUP):
            r_abs = g * NA_GROUP + qr
            win_r = min(max(r_abs - NA_KR // 2, 0), rows - NA_KR)
            pieces = []
            for kr in range(NA_WIN):
                key_r = base + kr
                inside = win_r <= key_r < win_r + NA_KR
                pieces.append(blocks[key_r - r_abs + NA_KR - 1] if inside else neg)
            bias_ref[kind, qr * w:(qr + 1) * w, :] = jnp.concatenate(pieces, axis=1)


def _na_kernel(q_ref, k_ref, v_ref, rpb_ref, o_ref, bias_ref, *, ctx_len, rows):
    scale = NA_HD ** -0.5 * LOG2E
    gq = NA_GROUP * GRID_W
    nwin = NA_WIN * GRID_W
    kc = k_ref[0:ctx_len, :]
    vc = v_ref[0:ctx_len, :]

    @pl.when(pl.program_id(1) == 0)
    def _():
        _na_build_bias(rpb_ref, bias_ref, rows)

    s = _dot_nt(q_ref[0:ctx_len, :], kc) * scale
    e = jnp.exp2(s - jnp.max(s, axis=-1, keepdims=True))
    pr = e / jnp.sum(e, axis=-1, keepdims=True)
    o_ref[0:ctx_len, :] = _dot(pr.astype(BF16), vc).astype(o_ref.dtype)

    ngroups = rows // NA_GROUP
    for g in range(ngroups):
        kind = 0 if g == 0 else (2 if g == ngroups - 1 else 1)
        q0 = ctx_len + g * gq
        k0 = ctx_len + _na_group_base(g, rows) * GRID_W
        qg = q_ref[q0:q0 + gq, :]
        s_loc = _dot_nt(qg, k_ref[k0:k0 + nwin, :]) * scale + bias_ref[kind]
        s_ctx = _dot_nt(qg, kc) * scale
        m = jnp.maximum(jnp.max(s_loc, axis=-1, keepdims=True), jnp.max(s_ctx, axis=-1, keepdims=True))
        e_loc = jnp.exp2(s_loc - m)
        e_ctx = jnp.exp2(s_ctx - m)
        den = jnp.sum(e_loc, axis=-1, keepdims=True) + jnp.sum(e_ctx, axis=-1, keepdims=True)
        o = _dot(e_loc.astype(BF16), v_ref[k0:k0 + nwin, :]) + _dot(e_ctx.astype(BF16), vc)
        o_ref[q0:q0 + gq, :] = (o / den).astype(o_ref.dtype)


def _na_branch(p, rpb, l, bsz, ctx_len, slab):
    rows = (slab - ctx_len) // GRID_W
    return pl.pallas_call(
        functools.partial(_na_kernel, ctx_len=ctx_len, rows=rows),
        grid=(NA_HEADS, bsz),
        in_specs=[pl.BlockSpec((slab, NA_HD), lambda h, b: (b, P_NQ // NA_HD + h)),
                  pl.BlockSpec((slab, NA_HD), lambda h, b: (b, P_NK // NA_HD + h)),
                  pl.BlockSpec((slab, NA_HD), lambda h, b: (b, P_NV // NA_HD + h)),
                  pl.BlockSpec((None, None) + rpb.shape[2:], lambda h, b: (l, h, 0, 0))],
        out_specs=pl.BlockSpec((slab, NA_HD), lambda h, b: (b, h)),
        out_shape=jax.ShapeDtypeStruct((bsz * slab, NA_W), BF16),
        scratch_shapes=[pltpu.VMEM((3, NA_GROUP * GRID_W, NA_WIN * GRID_W), F32)],
        compiler_params=_cparams(2),
        name="na_branch",
    )(p, p, p, rpb)


def _gla_dk_order():
    q = GLA_DK // 4
    one = np.full((GLA_DKP,), -1, np.int64)
    one[0:q] = np.arange(0, q)
    one[q:2 * q] = np.arange(2 * q, 3 * q)
    one[GLA_DKP // 2:GLA_DKP // 2 + q] = np.arange(q, 2 * q)
    one[GLA_DKP // 2 + q:GLA_DKP // 2 + 2 * q] = np.arange(3 * q, 4 * q)
    out = np.concatenate([np.where(one >= 0, one + h * GLA_DK, -1) for h in range(GLA_HEADS)])
    return out


def _take_cols(w, cols, axis=-1):
    axis = axis % w.ndim
    pieces = []
    i = 0
    n = len(cols)
    while i < n:
        j = i + 1
        if cols[i] < 0:
            while j < n and cols[j] < 0:
                j += 1
            pieces.append(jnp.zeros(w.shape[:axis] + (j - i,) + w.shape[axis + 1:], w.dtype))
        else:
            while j < n and cols[j] == cols[j - 1] + 1:
                j += 1
            pieces.append(lax.slice_in_dim(w, int(cols[i]), int(cols[i]) + (j - i), axis=axis))
        i = j
    return jnp.concatenate(pieces, axis=axis)


def _in_proj_offsets():
    o_q = 3 * CONV_W
    o_k = o_q + GLA_HEADS * GLA_DK
    o_v = o_k + GLA_HEADS * GLA_DK
    o_lr = o_v + 2 * GLA_HEADS * GLA_DV
    o_nq = o_lr + 2 * GLA_RANK
    return o_q, o_k, o_v, o_lr, o_nq


def _qk_weights(w_in_t):
    o_q, o_k, o_v, o_lr, o_nq = _in_proj_offsets()
    order = _gla_dk_order()
    q_order = np.where(order >= 0, order + o_q, -1)
    q_order[LR_LANE:LR_LANE + 2 * GLA_RANK] = o_lr + np.arange(2 * GLA_RANK)
    k_order = np.where(order >= 0, order + o_k, -1)
    return _take_cols(w_in_t, np.concatenate([q_order, k_order]), axis=1)


def _rope_tables(ctx_len, seq):
    q = GLA_DK // 4
    pos = jnp.arange(seq)
    freq = ROPE_THETA ** (-jnp.arange(q, dtype=F32) / q)
    ang_r = (pos // GRID_W).astype(F32)[:, None] * freq[None, :]
    ang_c = (pos % GRID_W).astype(F32)[:, None] * freq[None, :]
    pad = GLA_DKP // 2 - 2 * q
    cos = jnp.concatenate([jnp.cos(ang_r), jnp.cos(ang_c), jnp.ones((seq, pad), F32)], axis=1)
    sin = jnp.concatenate([jnp.sin(ang_r), jnp.sin(ang_c), jnp.zeros((seq, pad), F32)], axis=1)
    cos = jnp.concatenate([jnp.ones((ctx_len, GLA_DKP // 2), F32), cos], axis=0)
    sin = jnp.concatenate([jnp.zeros((ctx_len, GLA_DKP // 2), F32), sin], axis=0)
    return cos, sin


def kernel(x, c, ctx, c_ctx, mod_a, mod_b, mod_bias, norm_g, ffn_up, ffn_down, w_in, conv_w,
           gla_decay_w, gla_decay_b, gla_norm_g, na_rpb, w_branch_conv, w_branch_gla, w_branch_na,
           w_out, final_g):
    bsz, seq, d = x.shape
    ctx_len = ctx.shape[1]
    depth = mod_a.shape[0]
    slab = ctx_len + seq
    rows = seq // GRID_W
    assert slab % TM == 0 and d % TN == 0 and seq % ctx_len == 0
    assert rows % NA_GROUP == 0 and rows >= NA_WIN
    assert ctx_len % (2 * GLA_CHUNK) == 0 and seq % (2 * GLA_CHUNK) == 0

    mrows = 16
    cvec = jnp.concatenate([c_ctx[None, :], c, jnp.zeros((mrows - 1 - bsz, d), F32)], axis=0)
    mods = _modulation(cvec, mod_a, mod_b, mod_bias)[:, :1 + bsz].reshape(depth, 1 + bsz, 1, N_MOD * d)

    o_q, _, o_v, o_lr, o_nq = _in_proj_offsets()
    p_runs = [(0, o_q), (o_v, o_lr - o_v), (o_nq, w_in.shape[-1] - o_nq)]
    w_in_t = jnp.swapaxes(w_in, 1, 2)
    wq_t = _qk_weights(w_in_t)
    ffn_up_b = ffn_up.astype(BF16)
    ffn_down_b = ffn_down.astype(BF16).reshape(depth * 2, ffn_down.shape[2], d)
    wb_conv = w_branch_conv.astype(BF16)
    wb_gla = w_branch_gla.astype(BF16)
    wb_na = w_branch_na.astype(BF16)
    w_out_b = w_out.astype(BF16)
    order = _gla_dk_order()
    dw = _take_cols(gla_decay_w, order).reshape(depth, 2, GLA_RANK, GLA_HEADS, GLA_DKP)
    dw = dw.transpose(0, 1, 3, 2, 4)
    wdf = jnp.zeros((depth, GLA_HEADS, 128, GLA_DKP), F32).at[:, :, LR_LANE:LR_LANE + GLA_RANK].set(dw[:, 0])
    wdb = jnp.zeros((depth, GLA_HEADS, 128, GLA_DKP), F32).at[:, :, LR_LANE + GLA_RANK:LR_LANE + 2 * GLA_RANK].set(dw[:, 1])
    wdf = wdf.astype(BF16)
    wdb = wdb.astype(BF16)
    db = _take_cols(gla_decay_b, order).reshape(depth, 2, GLA_HEADS, 1, GLA_DKP)
    ng = gla_norm_g.reshape(depth, GLA_HEADS, 1, GLA_DV)
    cos, sin = _rope_tables(ctx_len, seq)
    rpb_pad = jnp.pad(na_rpb, ((0, 0), (0, 0), (0, 1), (0, 128 - na_rpb.shape[-1])))

    h = jnp.concatenate([ctx, x], axis=1).reshape(bsz * slab, d)
    for l in range(depth):
        m = mods[l]
        nm = functools.partial(_norm_mod, m=m, ctx_len=ctx_len, slab=slab)
        rp = functools.partial(_resid_proj, m=m, ctx_len=ctx_len, slab=slab)
        act = _ffn_up(nm(h, norm_g[l, 0:1], sub=0), ffn_up_b, l, 0)
        h = rp(act, ffn_down_b, 2 * l, h, sub=0, scale=0.5)
        u = nm(h, norm_g[l, 1:2], sub=1)
        p = _proj_runs(u, w_in_t, l, p_runs)
        pq = _proj_runs(u, wq_t, l, [(0, wq_t.shape[1])])
        ya = _conv_branch(p, conv_w[l], bsz, ctx_len, slab)
        yb = _gla_branch(p, pq, cos, sin, wdf[l], wdb[l], db[l, 0], db[l, 1], ng[l], bsz, ctx_len, slab)
        yc = _na_branch(p, rpb_pad, l, bsz, ctx_len, slab)
        z = _branch_proj(ya, yb, yc, wb_conv, wb_gla, wb_na, l, p)
        h = rp(z, w_out_b, l, h, sub=1, scale=1.0)
        act = _ffn_up(nm(h, norm_g[l, 2:3], sub=2), ffn_up_b, l, 1)
        h = rp(act, ffn_down_b, 2 * l + 1, h, sub=2, scale=0.5)
    return _final_norm(h, final_g.reshape(1, d), bsz, ctx_len, seq)
```

```python
import functools

import numpy as np
import jax
import jax.numpy as jnp
from jax import lax
from jax.experimental import pallas as pl
from jax.experimental.pallas import tpu as pltpu

F32 = jnp.float32
BF16 = jnp.bfloat16

GRID_W = 64
N_MOD = 9
CONV_W = 1024
GLA_HEADS = 4
GLA_DK = 192
GLA_DV = 384
GLA_DKP = 256
GLA_RANK = 16
GLA_TAU = 16.0
GLA_CHUNK = 64
NA_HEADS = 12
NA_HD = 128
NA_W = NA_HEADS * NA_HD
NA_KR = 8
NA_KC = 16
NA_GROUP = 4
NA_WIN = NA_KR + NA_GROUP - 1
ROPE_THETA = 10000.0
EPS = 1e-6
LOG2E = 1.4426950408889634

P_H, P_BG, P_CG = 0, 1024, 2048
P_V, P_G = 3072, 4608
P_NQ, P_NK, P_NV = 6144, 7680, 9216
P_GA = 10752
PQ_Q, PQ_K = 0, 1024
LR_LANE = 96

VMEM_LIMIT = 56 * 1024 * 1024
TM = 1152
TM_WIDE = 2 * TM
TN = 512
NORM_ROWS = 576
NORM_STEP_ROWS = 16


def _cparams(n_axes):
    return pltpu.CompilerParams(dimension_semantics=("arbitrary",) * n_axes,
                                vmem_limit_bytes=VMEM_LIMIT)


def _dot(a, b):
    return jnp.dot(a, b, preferred_element_type=F32)


def _dot_nt(a, b):
    return lax.dot_general(a, b, (((1,), (1,)), ((), ())), preferred_element_type=F32)


def _dot_tn(a, b):
    return lax.dot_general(a, b, (((0,), (0,)), ((), ())), preferred_element_type=F32)


def _sigmoid(x):
    return 0.5 * jnp.tanh(0.5 * x) + 0.5


def _mod_a_kernel(c_ref, a_ref, o_ref):
    c = c_ref[...]
    s = (c * jax.nn.sigmoid(c)).astype(BF16)
    o_ref[...] = _dot(s, a_ref[...].astype(BF16))


def _mod_b_kernel(t_ref, b_ref, bias_ref, o_ref):
    o_ref[...] = _dot(t_ref[...].astype(BF16), b_ref[...].astype(BF16)) + bias_ref[...]


def _modulation(cvec, mod_a, mod_b, mod_bias):
    depth, d, rank = mod_a.shape
    rows = cvec.shape[0]
    nm = mod_b.shape[-1]
    tn = min(d, 2048)
    assert nm % tn == 0
    t = pl.pallas_call(
        _mod_a_kernel,
        grid=(depth,),
        in_specs=[pl.BlockSpec((rows, d), lambda l: (0, 0)),
                  pl.BlockSpec((None, d, rank), lambda l: (l, 0, 0))],
        out_specs=pl.BlockSpec((None, rows, rank), lambda l: (l, 0, 0)),
        out_shape=jax.ShapeDtypeStruct((depth, rows, rank), F32),
        compiler_params=_cparams(1),
        name="mod_a",
    )(cvec, mod_a)
    return pl.pallas_call(
        _mod_b_kernel,
        grid=(depth, nm // tn),
        in_specs=[pl.BlockSpec((None, rows, rank), lambda l, j: (l, 0, 0)),
                  pl.BlockSpec((None, rank, tn), lambda l, j: (l, 0, j)),
                  pl.BlockSpec((None, 1, tn), lambda l, j: (l, 0, j))],
        out_specs=pl.BlockSpec((None, rows, tn), lambda l, j: (l, 0, j)),
        out_shape=jax.ShapeDtypeStruct((depth, rows, nm), F32),
        compiler_params=_cparams(2),
        name="mod_b",
    )(t, mod_b, mod_bias.reshape(depth, 1, nm))


def _norm_mod_kernel(h_ref, g_ref, shc_ref, scc_ref, shx_ref, scx_ref, o_ref, *, ctx_len, slab):
    tr = h_ref.shape[0]
    first = (pl.program_id(0) * tr) % slab == 0
    g = g_ref[...]
    gs_x = g * (1.0 + scx_ref[...])
    sh_x = shx_ref[...]
    gs_head = jnp.where(first, g * (1.0 + scc_ref[...]), gs_x)
    sh_head = jnp.where(first, shc_ref[...], sh_x)
    inv_d = 1.0 / h_ref.shape[1]
    for r in range(0, tr, NORM_STEP_ROWS):
        gs, sh = (gs_head, sh_head) if r < ctx_len else (gs_x, sh_x)
        x = h_ref[r:r + NORM_STEP_ROWS, :]
        rstd = lax.rsqrt(jnp.sum(x * x, axis=-1, keepdims=True) * inv_d + EPS)
        o_ref[r:r + NORM_STEP_ROWS, :] = ((h_ref[r:r + NORM_STEP_ROWS, :] * rstd) * gs + sh).astype(o_ref.dtype)


def _norm_mod(h, g, m, sub, ctx_len, slab):
    mrows, d = h.shape
    tr = NORM_ROWS
    assert slab % tr == 0 and tr >= ctx_len and ctx_len % NORM_STEP_ROWS == 0
    ctx_row = lambda c: (lambda j: (0, 0, c))
    batch_row = lambda c: (lambda j: (1 + (j * tr) // slab, 0, c))
    return pl.pallas_call(
        functools.partial(_norm_mod_kernel, ctx_len=ctx_len, slab=slab),
        grid=(mrows // tr,),
        in_specs=[pl.BlockSpec((tr, d), lambda j: (j, 0)),
                  pl.BlockSpec((1, d), lambda j: (0, 0)),
                  pl.BlockSpec((None, 1, d), ctx_row(3 * sub)),
                  pl.BlockSpec((None, 1, d), ctx_row(3 * sub + 1)),
                  pl.BlockSpec((None, 1, d), batch_row(3 * sub)),
                  pl.BlockSpec((None, 1, d), batch_row(3 * sub + 1))],
        out_specs=pl.BlockSpec((tr, d), lambda j: (j, 0)),
        out_shape=jax.ShapeDtypeStruct((mrows, d), BF16),
        compiler_params=_cparams(1),
        name="norm_mod",
    )(h, g, m, m, m, m)


def _final_norm_kernel(h_ref, g_ref, o_ref):
    g = g_ref[...]
    inv_d = 1.0 / h_ref.shape[1]
    for r in range(0, h_ref.shape[0], NORM_STEP_ROWS):
        x = h_ref[r:r + NORM_STEP_ROWS, :]
        rstd = lax.rsqrt(jnp.sum(x * x, axis=-1, keepdims=True) * inv_d + EPS)
        o_ref[r:r + NORM_STEP_ROWS, :] = (h_ref[r:r + NORM_STEP_ROWS, :] * rstd) * g


def _final_norm(h, g, bsz, ctx_len, seq):
    d = h.shape[-1]
    slab = ctx_len + seq
    tr = ctx_len
    return pl.pallas_call(
        _final_norm_kernel,
        grid=(bsz, seq // tr),
        in_specs=[pl.BlockSpec((None, tr, d), lambda b, j: (b, 1 + j, 0)),
                  pl.BlockSpec((1, d), lambda b, j: (0, 0))],
        out_specs=pl.BlockSpec((None, tr, d), lambda b, j: (b, j, 0)),
        out_shape=jax.ShapeDtypeStruct((bsz, seq, d), F32),
        compiler_params=_cparams(2),
        name="final_norm",
    )(h.reshape(bsz, slab, d), g)


def _proj_t_kernel(a_ref, wt_ref, o_ref):
    o_ref[...] = _dot_nt(a_ref[...], wt_ref[0].astype(BF16)).astype(o_ref.dtype)


def _proj_runs(a, wt, l, runs):
    mrows, k = a.shape
    assert all(s % 8 == 0 and n % TN == 0 for s, n in runs)
    starts = np.concatenate([s + TN * np.arange(n // TN) for s, n in runs])
    first = int(starts[0])
    step_at = [int(i) for i in np.nonzero(np.diff(starts) != TN)[0] + 1]
    jumps = [int(starts[i] - starts[i - 1] - TN) for i in step_at]

    def wrow(j):
        r = j * TN + first
        for at, jump in zip(step_at, jumps):
            r = r + jnp.where(j >= at, jump, 0)
        return pl.multiple_of(r, 8)

    wide = len(starts) >= 16
    tm = TM_WIDE if wide else TM
    a_spec = (pl.BlockSpec((tm, k), lambda i, j: (i, 0), pipeline_mode=pl.Buffered(1)) if wide
              else pl.BlockSpec((tm, k), lambda i, j: (i, 0)))
    return pl.pallas_call(
        _proj_t_kernel,
        grid=(mrows // tm, len(starts)),
        in_specs=[a_spec,
                  pl.BlockSpec((pl.Element(1), pl.Element(TN), pl.Element(k)), lambda i, j: (l, wrow(j), 0))],
        out_specs=pl.BlockSpec((tm, TN), lambda i, j: (i, j)),
        out_shape=jax.ShapeDtypeStruct((mrows, len(starts) * TN), BF16),
        compiler_params=_cparams(2),
        name="in_proj_runs",
    )(a, wt)


def _swiglu_kernel(u_ref, wa_ref, wb_ref, o_ref):
    u = u_ref[...]
    a = _dot(u, wa_ref[...])
    b = _dot(u, wb_ref[...])
    o_ref[...] = (a * _sigmoid(a) * b).astype(o_ref.dtype)


def _ffn_up(u, w_up, l, s):
    mrows, k = u.shape
    dff = w_up.shape[-1] // 2
    nb = dff // TN
    return pl.pallas_call(
        _swiglu_kernel,
        grid=(mrows // TM, nb),
        in_specs=[pl.BlockSpec((TM, k), lambda i, j: (i, 0)),
                  pl.BlockSpec((None, None, k, TN), lambda i, j: (l, s, 0, j)),
                  pl.BlockSpec((None, None, k, TN), lambda i, j: (l, s, 0, nb + j))],
        out_specs=pl.BlockSpec((TM, TN), lambda i, j: (i, j)),
        out_shape=jax.ShapeDtypeStruct((mrows, dff), BF16),
        compiler_params=_cparams(2),
        name="ffn_up",
    )(u, w_up, w_up)


def _resid_kernel(a_ref, w_ref, h_ref, gc_ref, gx_ref, o_ref, *, scale, ctx_len, slab):
    tm = a_ref.shape[0]
    acc = _dot(a_ref[...], w_ref[...])
    row0 = (pl.program_id(0) * tm) % slab
    rows = lax.broadcasted_iota(jnp.int32, (tm, 1), 0) + row0
    gate = jnp.where(rows < ctx_len, gc_ref[...], gx_ref[...])
    o_ref[...] = h_ref[...] + (scale * gate) * acc


def _resid_proj(a, w, widx, h, m, sub, scale, ctx_len, slab):
    mrows, k = a.shape
    n = w.shape[-1]
    g0 = (3 * sub + 2) * (n // TN)
    return pl.pallas_call(
        functools.partial(_resid_kernel, scale=scale, ctx_len=ctx_len, slab=slab),
        grid=(mrows // TM, n // TN),
        in_specs=[pl.BlockSpec((TM, k), lambda i, j: (i, 0)),
                  pl.BlockSpec((None, k, TN), lambda i, j: (widx, 0, j)),
                  pl.BlockSpec((TM, TN), lambda i, j: (i, j)),
                  pl.BlockSpec((None, 1, TN), lambda i, j: (0, 0, g0 + j)),
                  pl.BlockSpec((None, 1, TN), lambda i, j: (1 + (i * TM) // slab, 0, g0 + j))],
        out_specs=pl.BlockSpec((TM, TN), lambda i, j: (i, j)),
        out_shape=jax.ShapeDtypeStruct((mrows, n), F32),
        input_output_aliases={2: 0},
        compiler_params=_cparams(2),
        name="resid_proj",
    )(a, w, h, m, m)


def _branch_kernel(a_ref, b_ref, c_ref, pa_ref, pb_ref, pc_ref, ga_ref, gb_ref, gc_ref, o_ref):
    ya = _dot(a_ref[...], pa_ref[...])
    yb = _dot(b_ref[...], pb_ref[...])
    yc = _dot(c_ref[...], pc_ref[...])
    z = (_sigmoid(ga_ref[...].astype(F32)) * ya
         + _sigmoid(gb_ref[...].astype(F32)) * yb
         + _sigmoid(gc_ref[...].astype(F32)) * yc)
    o_ref[...] = z.astype(o_ref.dtype)


def _branch_proj(a, b, c, pa, pb, pc, l, p):
    mrows = a.shape[0]
    d = pa.shape[-1]
    tn = TN
    g0 = P_GA // tn
    nb = d // tn
    return pl.pallas_call(
        _branch_kernel,
        grid=(mrows // TM, nb),
        in_specs=[pl.BlockSpec((TM, a.shape[1]), lambda i, j: (i, 0)),
                  pl.BlockSpec((TM, b.shape[1]), lambda i, j: (i, 0)),
                  pl.BlockSpec((TM, c.shape[1]), lambda i, j: (i, 0)),
                  pl.BlockSpec((None, pa.shape[1], tn), lambda i, j: (l, 0, j)),
                  pl.BlockSpec((None, pb.shape[1], tn), lambda i, j: (l, 0, j)),
                  pl.BlockSpec((None, pc.shape[1], tn), lambda i, j: (l, 0, j)),
                  pl.BlockSpec((TM, tn), lambda i, j: (i, g0 + j)),
                  pl.BlockSpec((TM, tn), lambda i, j: (i, g0 + nb + j)),
                  pl.BlockSpec((TM, tn), lambda i, j: (i, g0 + 2 * nb + j))],
        out_specs=pl.BlockSpec((TM, tn), lambda i, j: (i, j)),
        out_shape=jax.ShapeDtypeStruct((mrows, d), BF16),
        compiler_params=_cparams(2),
        name="branch_proj",
    )(a, b, c, pa, pb, pc, p, p, p)


def _conv_kernel(h_ref, bg_ref, cg_ref, w_ref, o_ref, *, ctx_len):
    n = h_ref.shape[0]
    x = cg_ref[...].astype(F32) * h_ref[...].astype(F32)
    row = lax.broadcasted_iota(jnp.int32, (n, 1), 0)
    first = (row == 0) | (row == ctx_len)
    last = (row == ctx_len - 1) | (row == n - 1)
    prev = jnp.where(first, 0.0, pltpu.roll(x, 1, axis=0))
    nxt = jnp.where(last, 0.0, pltpu.roll(x, n - 1, axis=0))
    w = w_ref[...]
    y = prev * w[0:1, :] + x * w[1:2, :] + nxt * w[2:3, :]
    o_ref[...] = (bg_ref[...].astype(F32) * y).astype(o_ref.dtype)


def _conv_branch(p, conv_w, bsz, ctx_len, slab):
    tc = 256
    nb = CONV_W // tc
    return pl.pallas_call(
        functools.partial(_conv_kernel, ctx_len=ctx_len),
        grid=(bsz, nb),
        in_specs=[pl.BlockSpec((slab, tc), lambda b, j: (b, P_H // tc + j)),
                  pl.BlockSpec((slab, tc), lambda b, j: (b, P_BG // tc + j)),
                  pl.BlockSpec((slab, tc), lambda b, j: (b, P_CG // tc + j)),
                  pl.BlockSpec((conv_w.shape[0], tc), lambda b, j: (0, j))],
        out_specs=pl.BlockSpec((slab, tc), lambda b, j: (b, j)),
        out_shape=jax.ShapeDtypeStruct((bsz * slab, CONV_W), BF16),
        compiler_params=_cparams(2),
        name="conv_branch",
    )(p, p, p, conv_w)


def _log_sigmoid(x):
    return jnp.minimum(x, 0.0) - jnp.log(1.0 + jnp.exp(-jnp.abs(x)))


def _gla_kernel(q_ref, k_ref, v_ref, g_ref, lr_ref, cos_ref, sin_ref, wdf_ref, wdb_ref,
                bdf_ref, bdb_ref, ng_ref, o_ref, qd_s, q2_s, ki_s, ke_s, dec_s, oacc_s, st_s,
                *, ctx_len):
    n = q_ref.shape[0]
    pair = 2 * GLA_CHUNK
    npair = n // pair
    cpair = ctx_len // pair
    half = GLA_DKP // 2
    cos = cos_ref[...]
    sin = sin_ref[...]

    def rope(x):
        x1 = x[:, :half]
        x2 = x[:, half:]
        return jnp.concatenate([x1 * cos - x2 * sin, x1 * sin + x2 * cos], axis=1)

    qr = rope(q_ref[...].astype(F32)) * (GLA_DK ** -0.5)
    kr = rope(k_ref[...].astype(F32))
    lr = lr_ref[...]
    ridx = lax.broadcasted_iota(jnp.int32, (n, 1), 0) % GLA_CHUNK
    pi = lax.broadcasted_iota(jnp.int32, (pair, pair), 0)
    pj = lax.broadcasted_iota(jnp.int32, (pair, pair), 1)
    same = (pi < GLA_CHUNK) == (pj < GLA_CHUNK)

    def spread(t4):
        return jnp.broadcast_to(t4, (npair, 2, GLA_CHUNK, GLA_DKP)).reshape(n, GLA_DKP)

    def chunk_sums(la):
        cum = la
        for sft in (1, 2, 4, 8, 16, 32):
            cum = cum + jnp.where(ridx >= sft, pltpu.roll(cum, sft, axis=0), 0.0)
        tot4 = cum.reshape(npair, 2, GLA_CHUNK, GLA_DKP)[:, :, GLA_CHUNK - 1:GLA_CHUNK, :]
        return cum, tot4

    def stage(cum, tot4, forward):
        lo, hi = tot4[:, 0:1], tot4[:, 1:2]
        zero = jnp.zeros_like(lo)
        if forward:
            q_add = spread(jnp.concatenate([zero, lo], axis=1))
            k_add = spread(jnp.concatenate([hi, zero], axis=1))
        else:
            q_add = spread(jnp.concatenate([hi, zero], axis=1))
            k_add = spread(jnp.concatenate([zero, lo], axis=1))
        qd_s[...] = (qr * jnp.exp(cum)).astype(BF16)
        q2_s[...] = (qr * jnp.exp(cum + q_add)).astype(BF16)
        ki_s[...] = (kr * jnp.exp(-cum)).astype(BF16)
        ke_s[...] = (kr * jnp.exp(spread(tot4) - cum + k_add)).astype(BF16)
        dec_s[...] = jnp.exp(lo[:, 0] + hi[:, 0])

    def scan(pair_of, steps, diag, off, first):
        def body(i, carry):
            c = pair_of(i)
            rows = pl.ds(pl.multiple_of(c * pair, pair), pair)
            q2 = q2_s[rows, :]
            ki = ki_s[rows, :]
            vv = v_ref[rows, :]
            att = jnp.where(diag, _dot_nt(qd_s[rows, :], ki), jnp.where(off, _dot_nt(q2, ki), 0.0))
            st = st_s[...]
            o = _dot(att.astype(BF16), vv) + _dot_nt(q2, st.astype(BF16))
            if first:
                oacc_s[rows, :] = o
            else:
                oacc_s[rows, :] += o
            st_s[...] = st * dec_s[c] + _dot_tn(vv, ke_s[rows, :])
            return carry
        lax.fori_loop(0, steps, body, 0, unroll=2)

    la = _log_sigmoid(_dot(lr, wdf_ref[...]) + bdf_ref[...]) / GLA_TAU
    cum, tot4 = chunk_sums(la)
    stage(cum, tot4, True)
    st_s[...] = jnp.zeros_like(st_s)
    scan(lambda i: i, npair, same & (pi >= pj), pi > pj, True)

    la = _log_sigmoid(_dot(lr, wdb_ref[...]) + bdb_ref[...]) / GLA_TAU
    cum, tot4 = chunk_sums(la)
    stage(spread(tot4) - cum + la, tot4, False)
    st_s[...] = jnp.zeros_like(st_s)
    scan(lambda i: cpair - 1 - i, cpair, same & (pi <= pj), pi < pj, False)
    scan(lambda i: npair - 1 - i, npair - cpair, same & (pi <= pj), pi < pj, False)

    o = oacc_s[...]
    o = o * lax.rsqrt(jnp.mean(o * o, axis=-1, keepdims=True) + EPS) * ng_ref[...]
    g = g_ref[...].astype(F32)
    o_ref[...] = (o * (g * jax.nn.sigmoid(g))).astype(o_ref.dtype)


def _gla_branch(p1, pq, cos, sin, wdf, wdb, bdf, bdb, ng, bsz, ctx_len, slab):
    npair = slab // (2 * GLA_CHUNK)
    head = lambda b, h: (h, 0, 0)
    return pl.pallas_call(
        functools.partial(_gla_kernel, ctx_len=ctx_len),
        grid=(bsz, GLA_HEADS),
        in_specs=[pl.BlockSpec((slab, GLA_DKP), lambda b, h: (b, PQ_Q // GLA_DKP + h)),
                  pl.BlockSpec((slab, GLA_DKP), lambda b, h: (b, PQ_K // GLA_DKP + h)),
                  pl.BlockSpec((slab, GLA_DV), lambda b, h: (b, P_V // GLA_DV + h)),
                  pl.BlockSpec((slab, GLA_DV), lambda b, h: (b, P_G // GLA_DV + h)),
                  pl.BlockSpec((slab, 128), lambda b, h: (b, PQ_Q // 128)),
                  pl.BlockSpec((slab, GLA_DKP // 2), lambda b, h: (0, 0)),
                  pl.BlockSpec((slab, GLA_DKP // 2), lambda b, h: (0, 0)),
                  pl.BlockSpec((None, 128, GLA_DKP), head),
                  pl.BlockSpec((None, 128, GLA_DKP), head),
                  pl.BlockSpec((None, 1, GLA_DKP), head),
                  pl.BlockSpec((None, 1, GLA_DKP), head),
                  pl.BlockSpec((None, 1, GLA_DV), head)],
        out_specs=pl.BlockSpec((slab, GLA_DV), lambda b, h: (b, h)),
        out_shape=jax.ShapeDtypeStruct((bsz * slab, GLA_HEADS * GLA_DV), BF16),
        scratch_shapes=[pltpu.VMEM((slab, GLA_DKP), BF16),
                        pltpu.VMEM((slab, GLA_DKP), BF16),
                        pltpu.VMEM((slab, GLA_DKP), BF16),
                        pltpu.VMEM((slab, GLA_DKP), BF16),
                        pltpu.VMEM((npair, 1, GLA_DKP), F32),
                        pltpu.VMEM((slab, GLA_DV), F32),
                        pltpu.VMEM((GLA_DV, GLA_DKP), F32)],
        compiler_params=_cparams(2),
        name="gla_branch",
    )(pq, pq, p1, p1, pq, cos, sin, wdf, wdb, bdf, bdb, ng)


def _na_group_base(g, rows):
    return min(max(g * NA_GROUP - NA_KR // 2, 0), rows - NA_WIN)


def _na_build_bias(rpb_ref, bias_ref, rows):
    w = GRID_W
    qc = lax.broadcasted_iota(jnp.int32, (w, w), 0)
    kc = lax.broadcasted_iota(jnp.int32, (w, w), 1)
    win = jnp.clip(qc - NA_KC // 2, 0, w - NA_KC)
    col_ok = (kc >= win) & (kc < win + NA_KC)
    neg = jnp.full((w, w), -jnp.inf, F32)
    blocks = []
    for ro in range(2 * NA_KR - 1):
        row = jnp.broadcast_to(rpb_ref[ro:ro + 1, :], (w, 128))
        band = pltpu.roll(row, 128 - (NA_KC - 1), axis=1, stride=1, stride_axis=0)
        blocks.append(jnp.where(col_ok, band[:, :w] * LOG2E, neg))
    ngroups = rows // NA_GROUP
    for kind, g in enumerate((0, 1, ngroups - 1)):
        base = _na_group_base(g, rows)
        for qr in range(NA_GROUP):
            r_abs = g * NA_GROUP + qr
            win_r = min(max(r_abs - NA_KR // 2, 0), rows - NA_KR)
            pieces = []
            for kr in range(NA_WIN):
                key_r = base + kr
                inside = win_r <= key_r < win_r + NA_KR
                pieces.append(blocks[key_r - r_abs + NA_KR - 1] if inside else neg)
            bias_ref[kind, qr * w:(qr + 1) * w, :] = jnp.concatenate(pieces, axis=1)


def _na_kernel(q_ref, k_ref, v_ref, rpb_ref, o_ref, bias_ref, *, ctx_len, rows):
    scale = NA_HD ** -0.5 * LOG2E
    gq = NA_GROUP * GRID_W
    nwin = NA_WIN * GRID_W
    kc = k_ref[0:ctx_len, :]
    vc = v_ref[0:ctx_len, :]

    @pl.when(pl.program_id(1) == 0)
    def _():
        _na_build_bias(rpb_ref, bias_ref, rows)

    s = _dot_nt(q_ref[0:ctx_len, :], kc) * scale
    e = jnp.exp2(s - jnp.max(s, axis=-1, keepdims=True))
    pr = e / jnp.sum(e, axis=-1, keepdims=True)
    o_ref[0:ctx_len, :] = _dot(pr.astype(BF16), vc).astype(o_ref.dtype)

    ngroups = rows // NA_GROUP
    for g in range(ngroups):
        kind = 0 if g == 0 else (2 if g == ngroups - 1 else 1)
        q0 = ctx_len + g * gq
        k0 = ctx_len + _na_group_base(g, rows) * GRID_W
        qg = q_ref[q0:q0 + gq, :]
        s_loc = _dot_nt(qg, k_ref[k0:k0 + nwin, :]) * scale + bias_ref[kind]
        s_ctx = _dot_nt(qg, kc) * scale
        m = jnp.maximum(jnp.max(s_loc, axis=-1, keepdims=True), jnp.max(s_ctx, axis=-1, keepdims=True))
        e_loc = jnp.exp2(s_loc - m)
        e_ctx = jnp.exp2(s_ctx - m)
        den = jnp.sum(e_loc, axis=-1, keepdims=True) + jnp.sum(e_ctx, axis=-1, keepdims=True)
        o = _dot(e_loc.astype(BF16), v_ref[k0:k0 + nwin, :]) + _dot(e_ctx.astype(BF16), vc)
        o_ref[q0:q0 + gq, :] = (o / den).astype(o_ref.dtype)


def _na_branch(p, rpb, l, bsz, ctx_len, slab):
    rows = (slab - ctx_len) // GRID_W
    return pl.pallas_call(
        functools.partial(_na_kernel, ctx_len=ctx_len, rows=rows),
        grid=(NA_HEADS, bsz),
        in_specs=[pl.BlockSpec((slab, NA_HD), lambda h, b: (b, P_NQ // NA_HD + h)),
                  pl.BlockSpec((slab, NA_HD), lambda h, b: (b, P_NK // NA_HD + h)),
                  pl.BlockSpec((slab, NA_HD), lambda h, b: (b, P_NV // NA_HD + h)),
                  pl.BlockSpec((None, None) + rpb.shape[2:], lambda h, b: (l, h, 0, 0))],
        out_specs=pl.BlockSpec((slab, NA_HD), lambda h, b: (b, h)),
        out_shape=jax.ShapeDtypeStruct((bsz * slab, NA_W), BF16),
        scratch_shapes=[pltpu.VMEM((3, NA_GROUP * GRID_W, NA_WIN * GRID_W), F32)],
        compiler_params=_cparams(2),
        name="na_branch",
    )(p, p, p, rpb)


def _gla_dk_order():
    q = GLA_DK // 4
    one = np.full((GLA_DKP,), -1, np.int64)
    one[0:q] = np.arange(0, q)
    one[q:2 * q] = np.arange(2 * q, 3 * q)
    one[GLA_DKP // 2:GLA_DKP // 2 + q] = np.arange(q, 2 * q)
    one[GLA_DKP // 2 + q:GLA_DKP // 2 + 2 * q] = np.arange(3 * q, 4 * q)
    out = np.concatenate([np.where(one >= 0, one + h * GLA_DK, -1) for h in range(GLA_HEADS)])
    return out


def _take_cols(w, cols, axis=-1):
    axis = axis % w.ndim
    pieces = []
    i = 0
    n = len(cols)
    while i < n:
        j = i + 1
        if cols[i] < 0:
            while j < n and cols[j] < 0:
                j += 1
            pieces.append(jnp.zeros(w.shape[:axis] + (j - i,) + w.shape[axis + 1:], w.dtype))
        else:
            while j < n and cols[j] == cols[j - 1] + 1:
                j += 1
            pieces.append(lax.slice_in_dim(w, int(cols[i]), int(cols[i]) + (j - i), axis=axis))
        i = j
    return jnp.concatenate(pieces, axis=axis)


def _in_proj_offsets():
    o_q = 3 * CONV_W
    o_k = o_q + GLA_HEADS * GLA_DK
    o_v = o_k + GLA_HEADS * GLA_DK
    o_lr = o_v + 2 * GLA_HEADS * GLA_DV
    o_nq = o_lr + 2 * GLA_RANK
    return o_q, o_k, o_v, o_lr, o_nq


def _qk_weights(w_in_t):
    o_q, o_k, o_v, o_lr, o_nq = _in_proj_offsets()
    order = _gla_dk_order()
    q_order = np.where(order >= 0, order + o_q, -1)
    q_order[LR_LANE:LR_LANE + 2 * GLA_RANK] = o_lr + np.arange(2 * GLA_RANK)
    k_order = np.where(order >= 0, order + o_k, -1)
    return _take_cols(w_in_t, np.concatenate([q_order, k_order]), axis=1)


def _rope_tables(ctx_len, seq):
    q = GLA_DK // 4
    pos = jnp.arange(seq)
    freq = ROPE_THETA ** (-jnp.arange(q, dtype=F32) / q)
    ang_r = (pos // GRID_W).astype(F32)[:, None] * freq[None, :]
    ang_c = (pos % GRID_W).astype(F32)[:, None] * freq[None, :]
    pad = GLA_DKP // 2 - 2 * q
    cos = jnp.concatenate([jnp.cos(ang_r), jnp.cos(ang_c), jnp.ones((seq, pad), F32)], axis=1)
    sin = jnp.concatenate([jnp.sin(ang_r), jnp.sin(ang_c), jnp.zeros((seq, pad), F32)], axis=1)
    cos = jnp.concatenate([jnp.ones((ctx_len, GLA_DKP // 2), F32), cos], axis=0)
    sin = jnp.concatenate([jnp.zeros((ctx_len, GLA_DKP // 2), F32), sin], axis=0)
    return cos, sin


def kernel(x, c, ctx, c_ctx, mod_a, mod_b, mod_bias, norm_g, ffn_up, ffn_down, w_in, conv_w,
           gla_decay_w, gla_decay_b, gla_norm_g, na_rpb, w_branch_conv, w_branch_gla, w_branch_na,
           w_out, final_g):
    bsz, seq, d = x.shape
    ctx_len = ctx.shape[1]
    depth = mod_a.shape[0]
    slab = ctx_len + seq
    rows = seq // GRID_W
    assert slab % TM == 0 and d % TN == 0 and seq % ctx_len == 0
    assert rows % NA_GROUP == 0 and rows >= NA_WIN
    assert ctx_len % (2 * GLA_CHUNK) == 0 and seq % (2 * GLA_CHUNK) == 0

    mrows = 16
    cvec = jnp.concatenate([c_ctx[None, :], c, jnp.zeros((mrows - 1 - bsz, d), F32)], axis=0)
    mods = _modulation(cvec, mod_a, mod_b, mod_bias)[:, :1 + bsz].reshape(depth, 1 + bsz, 1, N_MOD * d)

    o_q, _, o_v, o_lr, o_nq = _in_proj_offsets()
    p_runs = [(0, o_q), (o_v, o_lr - o_v), (o_nq, w_in.shape[-1] - o_nq)]
    w_in_t = jnp.swapaxes(w_in, 1, 2)
    wq_t = _qk_weights(w_in_t)
    ffn_up_b = ffn_up.astype(BF16)
    ffn_down_b = ffn_down.astype(BF16).reshape(depth * 2, ffn_down.shape[2], d)
    wb_conv = w_branch_conv.astype(BF16)
    wb_gla = w_branch_gla.astype(BF16)
    wb_na = w_branch_na.astype(BF16)
    w_out_b = w_out.astype(BF16)
    order = _gla_dk_order()
    dw = _take_cols(gla_decay_w, order).reshape(depth, 2, GLA_RANK, GLA_HEADS, GLA_DKP)
    dw = dw.transpose(0, 1, 3, 2, 4)
    wdf = jnp.zeros((depth, GLA_HEADS, 128, GLA_DKP), F32).at[:, :, LR_LANE:LR_LANE + GLA_RANK].set(dw[:, 0])
    wdb = jnp.zeros((depth, GLA_HEADS, 128, GLA_DKP), F32).at[:, :, LR_LANE + GLA_RANK:LR_LANE + 2 * GLA_RANK].set(dw[:, 1])
    wdf = wdf.astype(BF16)
    wdb = wdb.astype(BF16)
    db = _take_cols(gla_decay_b, order).reshape(depth, 2, GLA_HEADS, 1, GLA_DKP)
    ng = gla_norm_g.reshape(depth, GLA_HEADS, 1, GLA_DV)
    cos, sin = _rope_tables(ctx_len, seq)
    rpb_pad = jnp.pad(na_rpb, ((0, 0), (0, 0), (0, 1), (0, 128 - na_rpb.shape[-1])))

    h = jnp.concatenate([ctx, x], axis=1).reshape(bsz * slab, d)
    for l in range(depth):
        m = mods[l]
        nm = functools.partial(_norm_mod, m=m, ctx_len=ctx_len, slab=slab)
        rp = functools.partial(_resid_proj, m=m, ctx_len=ctx_len, slab=slab)
        act = _ffn_up(nm(h, norm_g[l, 0:1], sub=0), ffn_up_b, l, 0)
        h = rp(act, ffn_down_b, 2 * l, h, sub=0, scale=0.5)
        u = nm(h, norm_g[l, 1:2], sub=1)
        p = _proj_runs(u, w_in_t, l, p_runs)
        pq = _proj_runs(u, wq_t, l, [(0, wq_t.shape[1])])
        ya = _conv_branch(p, conv_w[l], bsz, ctx_len, slab)
        yb = _gla_branch(p, pq, cos, sin, wdf[l], wdb[l], db[l, 0], db[l, 1], ng[l], bsz, ctx_len, slab)
        yc = _na_branch(p, rpb_pad, l, bsz, ctx_len, slab)
        z = _branch_proj(ya, yb, yc, wb_conv, wb_gla, wb_na, l, p)
        h = rp(z, w_out_b, l, h, sub=1, scale=1.0)
        act = _ffn_up(nm(h, norm_g[l, 2:3], sub=2), ffn_up_b, l, 1)
        h = rp(act, ffn_down_b, 2 * l + 1, h, sub=2, scale=0.5)
    return _final_norm(h, final_g.reshape(1, d), bsz, ctx_len, seq)
```

```python
import functools

import numpy as np
import jax
import jax.numpy as jnp
from jax import lax
from jax.experimental import pallas as pl
from jax.experimental.pallas import tpu as pltpu

F32 = jnp.float32
BF16 = jnp.bfloat16

GRID_W = 64
N_MOD = 9
CONV_W = 1024
GLA_HEADS = 4
GLA_DK = 192
GLA_DV = 384
GLA_DKP = 256
GLA_RANK = 16
GLA_TAU = 16.0
GLA_CHUNK = 64
NA_HEADS = 12
NA_HD = 128
NA_W = NA_HEADS * NA_HD
NA_KR = 8
NA_KC = 16
NA_GROUP = 4
NA_WIN = NA_KR + NA_GROUP - 1
ROPE_THETA = 10000.0
EPS = 1e-6
LOG2E = 1.4426950408889634

P_H, P_BG, P_CG = 0, 1024, 2048
P_V, P_G = 3072, 4608
P_NQ, P_NK, P_NV = 6144, 7680, 9216
P_GA = 10752
PQ_Q, PQ_K = 0, 1024
LR_LANE = 96

VMEM_LIMIT = 56 * 1024 * 1024
TM = 1152
TM_WIDE = 2 * TM
TN = 512
NORM_ROWS = 576
NORM_STEP_ROWS = 16


def _cparams(n_axes):
    return pltpu.CompilerParams(dimension_semantics=("arbitrary",) * n_axes,
                                vmem_limit_bytes=VMEM_LIMIT)


def _dot(a, b):
    return jnp.dot(a, b, preferred_element_type=F32)


def _dot_nt(a, b):
    return lax.dot_general(a, b, (((1,), (1,)), ((), ())), preferred_element_type=F32)


def _dot_tn(a, b):
    return lax.dot_general(a, b, (((0,), (0,)), ((), ())), preferred_element_type=F32)


def _sigmoid(x):
    return 0.5 * jnp.tanh(0.5 * x) + 0.5


def _mod_a_kernel(c_ref, a_ref, o_ref):
    c = c_ref[...]
    s = (c * jax.nn.sigmoid(c)).astype(BF16)
    o_ref[...] = _dot(s, a_ref[...].astype(BF16))


def _mod_b_kernel(t_ref, b_ref, bias_ref, o_ref):
    o_ref[...] = _dot(t_ref[...].astype(BF16), b_ref[...].astype(BF16)) + bias_ref[...]


def _modulation(cvec, mod_a, mod_b, mod_bias):
    depth, d, rank = mod_a.shape
    rows = cvec.shape[0]
    nm = mod_b.shape[-1]
    tn = min(d, 2048)
    assert nm % tn == 0
    t = pl.pallas_call(
        _mod_a_kernel,
        grid=(depth,),
        in_specs=[pl.BlockSpec((rows, d), lambda l: (0, 0)),
                  pl.BlockSpec((None, d, rank), lambda l: (l, 0, 0))],
        out_specs=pl.BlockSpec((None, rows, rank), lambda l: (l, 0, 0)),
        out_shape=jax.ShapeDtypeStruct((depth, rows, rank), F32),
        compiler_params=_cparams(1),
        name="mod_a",
    )(cvec, mod_a)
    return pl.pallas_call(
        _mod_b_kernel,
        grid=(depth, nm // tn),
        in_specs=[pl.BlockSpec((None, rows, rank), lambda l, j: (l, 0, 0)),
                  pl.BlockSpec((None, rank, tn), lambda l, j: (l, 0, j)),
                  pl.BlockSpec((None, 1, tn), lambda l, j: (l, 0, j))],
        out_specs=pl.BlockSpec((None, rows, tn), lambda l, j: (l, 0, j)),
        out_shape=jax.ShapeDtypeStruct((depth, rows, nm), F32),
        compiler_params=_cparams(2),
        name="mod_b",
    )(t, mod_b, mod_bias.reshape(depth, 1, nm))


def _norm_mod_kernel(h_ref, g_ref, shc_ref, scc_ref, shx_ref, scx_ref, o_ref, *, ctx_len, slab):
    tr = h_ref.shape[0]
    first = (pl.program_id(0) * tr) % slab == 0
    g = g_ref[...]
    gs_x = g * (1.0 + scx_ref[...])
    sh_x = shx_ref[...]
    gs_head = jnp.where(first, g * (1.0 + scc_ref[...]), gs_x)
    sh_head = jnp.where(first, shc_ref[...], sh_x)
    inv_d = 1.0 / h_ref.shape[1]
    for r in range(0, tr, NORM_STEP_ROWS):
        gs, sh = (gs_head, sh_head) if r < ctx_len else (gs_x, sh_x)
        x = h_ref[r:r + NORM_STEP_ROWS, :]
        rstd = lax.rsqrt(jnp.sum(x * x, axis=-1, keepdims=True) * inv_d + EPS)
        o_ref[r:r + NORM_STEP_ROWS, :] = ((h_ref[r:r + NORM_STEP_ROWS, :] * rstd) * gs + sh).astype(o_ref.dtype)


def _norm_mod(h, g, m, sub, ctx_len, slab):
    mrows, d = h.shape
    tr = NORM_ROWS
    assert slab % tr == 0 and tr >= ctx_len and ctx_len % NORM_STEP_ROWS == 0
    ctx_row = lambda c: (lambda j: (0, 0, c))
    batch_row = lambda c: (lambda j: (1 + (j * tr) // slab, 0, c))
    return pl.pallas_call(
        functools.partial(_norm_mod_kernel, ctx_len=ctx_len, slab=slab),
        grid=(mrows // tr,),
        in_specs=[pl.BlockSpec((tr, d), lambda j: (j, 0)),
                  pl.BlockSpec((1, d), lambda j: (0, 0)),
                  pl.BlockSpec((None, 1, d), ctx_row(3 * sub)),
                  pl.BlockSpec((None, 1, d), ctx_row(3 * sub + 1)),
                  pl.BlockSpec((None, 1, d), batch_row(3 * sub)),
                  pl.BlockSpec((None, 1, d), batch_row(3 * sub + 1))],
        out_specs=pl.BlockSpec((tr, d), lambda j: (j, 0)),
        out_shape=jax.ShapeDtypeStruct((mrows, d), BF16),
        compiler_params=_cparams(1),
        name="norm_mod",
    )(h, g, m, m, m, m)


def _final_norm_kernel(h_ref, g_ref, o_ref):
    g = g_ref[...]
    inv_d = 1.0 / h_ref.shape[1]
    for r in range(0, h_ref.shape[0], NORM_STEP_ROWS):
        x = h_ref[r:r + NORM_STEP_ROWS, :]
        rstd = lax.rsqrt(jnp.sum(x * x, axis=-1, keepdims=True) * inv_d + EPS)
        o_ref[r:r + NORM_STEP_ROWS, :] = (h_ref[r:r + NORM_STEP_ROWS, :] * rstd) * g


def _final_norm(h, g, bsz, ctx_len, seq):
    d = h.shape[-1]
    slab = ctx_len + seq
    tr = ctx_len
    return pl.pallas_call(
        _final_norm_kernel,
        grid=(bsz, seq // tr),
        in_specs=[pl.BlockSpec((None, tr, d), lambda b, j: (b, 1 + j, 0)),
                  pl.BlockSpec((1, d), lambda b, j: (0, 0))],
        out_specs=pl.BlockSpec((None, tr, d), lambda b, j: (b, j, 0)),
        out_shape=jax.ShapeDtypeStruct((bsz, seq, d), F32),
        compiler_params=_cparams(2),
        name="final_norm",
    )(h.reshape(bsz, slab, d), g)


def _proj_t_kernel(a_ref, wt_ref, o_ref):
    o_ref[...] = _dot_nt(a_ref[...], wt_ref[0].astype(BF16)).astype(o_ref.dtype)


def _proj_runs(a, wt, l, runs):
    mrows, k = a.shape
    assert all(s % 8 == 0 and n % TN == 0 for s, n in runs)
    starts = np.concatenate([s + TN * np.arange(n // TN) for s, n in runs])
    first = int(starts[0])
    step_at = [int(i) for i in np.nonzero(np.diff(starts) != TN)[0] + 1]
    jumps = [int(starts[i] - starts[i - 1] - TN) for i in step_at]

    def wrow(j):
        r = j * TN + first
        for at, jump in zip(step_at, jumps):
            r = r + jnp.where(j >= at, jump, 0)
        return pl.multiple_of(r, 8)

    wide = len(starts) >= 16
    tm = TM_WIDE if wide else TM
    a_spec = (pl.BlockSpec((tm, k), lambda i, j: (i, 0), pipeline_mode=pl.Buffered(1)) if wide
              else pl.BlockSpec((tm, k), lambda i, j: (i, 0)))
    return pl.pallas_call(
        _proj_t_kernel,
        grid=(mrows // tm, len(starts)),
        in_specs=[a_spec,
                  pl.BlockSpec((pl.Element(1), pl.Element(TN), pl.Element(k)), lambda i, j: (l, wrow(j), 0))],
        out_specs=pl.BlockSpec((tm, TN), lambda i, j: (i, j)),
        out_shape=jax.ShapeDtypeStruct((mrows, len(starts) * TN), BF16),
        compiler_params=_cparams(2),
        name="in_proj_runs",
    )(a, wt)


def _swiglu_kernel(u_ref, wa_ref, wb_ref, o_ref):
    u = u_ref[...]
    a = _dot(u, wa_ref[...])
    b = _dot(u, wb_ref[...])
    o_ref[...] = (a * _sigmoid(a) * b).astype(o_ref.dtype)


def _ffn_up(u, w_up, l, s):
    mrows, k = u.shape
    dff = w_up.shape[-1] // 2
    nb = dff // TN
    return pl.pallas_call(
        _swiglu_kernel,
        grid=(mrows // TM, nb),
        in_specs=[pl.BlockSpec((TM, k), lambda i, j: (i, 0)),
                  pl.BlockSpec((None, None, k, TN), lambda i, j: (l, s, 0, j)),
                  pl.BlockSpec((None, None, k, TN), lambda i, j: (l, s, 0, nb + j))],
        out_specs=pl.BlockSpec((TM, TN), lambda i, j: (i, j)),
        out_shape=jax.ShapeDtypeStruct((mrows, dff), BF16),
        compiler_params=_cparams(2),
        name="ffn_up",
    )(u, w_up, w_up)


def _resid_kernel(a_ref, w_ref, h_ref, gc_ref, gx_ref, o_ref, *, scale, ctx_len, slab):
    tm = a_ref.shape[0]
    acc = _dot(a_ref[...], w_ref[...])
    row0 = (pl.program_id(0) * tm) % slab
    rows = lax.broadcasted_iota(jnp.int32, (tm, 1), 0) + row0
    gate = jnp.where(rows < ctx_len, gc_ref[...], gx_ref[...])
    o_ref[...] = h_ref[...] + (scale * gate) * acc


def _resid_proj(a, w, widx, h, m, sub, scale, ctx_len, slab):
    mrows, k = a.shape
    n = w.shape[-1]
    g0 = (3 * sub + 2) * (n // TN)
    return pl.pallas_call(
        functools.partial(_resid_kernel, scale=scale, ctx_len=ctx_len, slab=slab),
        grid=(mrows // TM, n // TN),
        in_specs=[pl.BlockSpec((TM, k), lambda i, j: (i, 0)),
                  pl.BlockSpec((None, k, TN), lambda i, j: (widx, 0, j)),
                  pl.BlockSpec((TM, TN), lambda i, j: (i, j)),
                  pl.BlockSpec((None, 1, TN), lambda i, j: (0, 0, g0 + j)),
                  pl.BlockSpec((None, 1, TN), lambda i, j: (1 + (i * TM) // slab, 0, g0 + j))],
        out_specs=pl.BlockSpec((TM, TN), lambda i, j: (i, j)),
        out_shape=jax.ShapeDtypeStruct((mrows, n), F32),
        input_output_aliases={2: 0},
        compiler_params=_cparams(2),
        name="resid_proj",
    )(a, w, h, m, m)


def _branch_kernel(a_ref, b_ref, c_ref, pa_ref, pb_ref, pc_ref, ga_ref, gb_ref, gc_ref, o_ref):
    ya = _dot(a_ref[...], pa_ref[...])
    yb = _dot(b_ref[...], pb_ref[...])
    yc = _dot(c_ref[...], pc_ref[...])
    z = (_sigmoid(ga_ref[...].astype(F32)) * ya
         + _sigmoid(gb_ref[...].astype(F32)) * yb
         + _sigmoid(gc_ref[...].astype(F32)) * yc)
    o_ref[...] = z.astype(o_ref.dtype)


def _branch_proj(a, b, c, pa, pb, pc, l, p):
    mrows = a.shape[0]
    d = pa.shape[-1]
    tn = TN
    g0 = P_GA // tn
    nb = d // tn
    return pl.pallas_call(
        _branch_kernel,
        grid=(mrows // TM, nb),
        in_specs=[pl.BlockSpec((TM, a.shape[1]), lambda i, j: (i, 0)),
                  pl.BlockSpec((TM, b.shape[1]), lambda i, j: (i, 0)),
                  pl.BlockSpec((TM, c.shape[1]), lambda i, j: (i, 0)),
                  pl.BlockSpec((None, pa.shape[1], tn), lambda i, j: (l, 0, j)),
                  pl.BlockSpec((None, pb.shape[1], tn), lambda i, j: (l, 0, j)),
                  pl.BlockSpec((None, pc.shape[1], tn), lambda i, j: (l, 0, j)),
                  pl.BlockSpec((TM, tn), lambda i, j: (i, g0 + j)),
                  pl.BlockSpec((TM, tn), lambda i, j: (i, g0 + nb + j)),
                  pl.BlockSpec((TM, tn), lambda i, j: (i, g0 + 2 * nb + j))],
        out_specs=pl.BlockSpec((TM, tn), lambda i, j: (i, j)),
        out_shape=jax.ShapeDtypeStruct((mrows, d), BF16),
        compiler_params=_cparams(2),
        name="branch_proj",
    )(a, b, c, pa, pb, pc, p, p, p)


def _conv_kernel(h_ref, bg_ref, cg_ref, w_ref, o_ref, *, ctx_len):
    n = h_ref.shape[0]
    x = cg_ref[...].astype(F32) * h_ref[...].astype(F32)
    row = lax.broadcasted_iota(jnp.int32, (n, 1), 0)
    first = (row == 0) | (row == ctx_len)
    last = (row == ctx_len - 1) | (row == n - 1)
    prev = jnp.where(first, 0.0, pltpu.roll(x, 1, axis=0))
    nxt = jnp.where(last, 0.0, pltpu.roll(x, n - 1, axis=0))
    w = w_ref[...]
    y = prev * w[0:1, :] + x * w[1:2, :] + nxt * w[2:3, :]
    o_ref[...] = (bg_ref[...].astype(F32) * y).astype(o_ref.dtype)


def _conv_branch(p, conv_w, bsz, ctx_len, slab):
    tc = 256
    nb = CONV_W // tc
    return pl.pallas_call(
        functools.partial(_conv_kernel, ctx_len=ctx_len),
        grid=(bsz, nb),
        in_specs=[pl.BlockSpec((slab, tc), lambda b, j: (b, P_H // tc + j)),
                  pl.BlockSpec((slab, tc), lambda b, j: (b, P_BG // tc + j)),
                  pl.BlockSpec((slab, tc), lambda b, j: (b, P_CG // tc + j)),
                  pl.BlockSpec((conv_w.shape[0], tc), lambda b, j: (0, j))],
        out_specs=pl.BlockSpec((slab, tc), lambda b, j: (b, j)),
        out_shape=jax.ShapeDtypeStruct((bsz * slab, CONV_W), BF16),
        compiler_params=_cparams(2),
        name="conv_branch",
    )(p, p, p, conv_w)


def _log_sigmoid(x):
    return jnp.minimum(x, 0.0) - jnp.log(1.0 + jnp.exp(-jnp.abs(x)))


def _gla_kernel(q_ref, k_ref, v_ref, g_ref, lr_ref, cos_ref, sin_ref, wdf_ref, wdb_ref,
                bdf_ref, bdb_ref, ng_ref, o_ref, qd_s, q2_s, ki_s, ke_s, dec_s, oacc_s, st_s,
                *, ctx_len):
    n = q_ref.shape[0]
    pair = 2 * GLA_CHUNK
    npair = n // pair
    cpair = ctx_len // pair
    half = GLA_DKP // 2
    cos = cos_ref[...]
    sin = sin_ref[...]

    def rope(x):
        x1 = x[:, :half]
        x2 = x[:, half:]
        return jnp.concatenate([x1 * cos - x2 * sin, x1 * sin + x2 * cos], axis=1)

    qr = rope(q_ref[...].astype(F32)) * (GLA_DK ** -0.5)
    kr = rope(k_ref[...].astype(F32))
    lr = lr_ref[...]
    ridx = lax.broadcasted_iota(jnp.int32, (n, 1), 0) % GLA_CHUNK
    pi = lax.broadcasted_iota(jnp.int32, (pair, pair), 0)
    pj = lax.broadcasted_iota(jnp.int32, (pair, pair), 1)
    same = (pi < GLA_CHUNK) == (pj < GLA_CHUNK)

    def spread(t4):
        return jnp.broadcast_to(t4, (npair, 2, GLA_CHUNK, GLA_DKP)).reshape(n, GLA_DKP)

    def chunk_sums(la):
        cum = la
        for sft in (1, 2, 4, 8, 16, 32):
            cum = cum + jnp.where(ridx >= sft, pltpu.roll(cum, sft, axis=0), 0.0)
        tot4 = cum.reshape(npair, 2, GLA_CHUNK, GLA_DKP)[:, :, GLA_CHUNK - 1:GLA_CHUNK, :]
        return cum, tot4

    def stage(cum, tot4, forward):
        lo, hi = tot4[:, 0:1], tot4[:, 1:2]
        zero = jnp.zeros_like(lo)
        if forward:
            q_add = spread(jnp.concatenate([zero, lo], axis=1))
            k_add = spread(jnp.concatenate([hi, zero], axis=1))
        else:
            q_add = spread(jnp.concatenate([hi, zero], axis=1))
            k_add = spread(jnp.concatenate([zero, lo], axis=1))
        d = 0 if forward else 1
        qd_s[d] = (qr * jnp.exp(cum)).astype(BF16)
        q2_s[d] = (qr * jnp.exp(cum + q_add)).astype(BF16)
        ki_s[d] = (kr * jnp.exp(-cum)).astype(BF16)
        ke_s[d] = (kr * jnp.exp(spread(tot4) - cum + k_add)).astype(BF16)
        dec_s[d] = jnp.exp(lo[:, 0] + hi[:, 0])

    masks = ((same & (pi >= pj), pi > pj), (same & (pi <= pj), pi < pj))

    def step(d, c):
        rows = pl.ds(pl.multiple_of(c * pair, pair), pair)
        diag, off = masks[d]
        q2 = q2_s[d, rows, :]
        ki = ki_s[d, rows, :]
        vv = v_ref[rows, :]
        att = jnp.where(diag, _dot_nt(qd_s[d, rows, :], ki), jnp.where(off, _dot_nt(q2, ki), 0.0))
        st = st_s[d]
        oacc_s[d, rows, :] = _dot(att.astype(BF16), vv) + _dot_nt(q2, st.astype(BF16))
        st_s[d] = st * dec_s[d, c] + _dot_tn(vv, ke_s[d, rows, :])

    la = _log_sigmoid(_dot(lr, wdf_ref[...]) + bdf_ref[...]) / GLA_TAU
    cum, tot4 = chunk_sums(la)
    stage(cum, tot4, True)
    la = _log_sigmoid(_dot(lr, wdb_ref[...]) + bdb_ref[...]) / GLA_TAU
    cum, tot4 = chunk_sums(la)
    stage(spread(tot4) - cum + la, tot4, False)
    st_s[...] = jnp.zeros_like(st_s)

    def body(i, carry):
        step(0, i)
        step(1, jnp.where(i < cpair, cpair - 1 - i, npair + cpair - 1 - i))
        return carry

    lax.fori_loop(0, npair, body, 0, unroll=2)

    o = oacc_s[0] + oacc_s[1]
    o = o * lax.rsqrt(jnp.mean(o * o, axis=-1, keepdims=True) + EPS) * ng_ref[...]
    g = g_ref[...].astype(F32)
    o_ref[...] = (o * (g * jax.nn.sigmoid(g))).astype(o_ref.dtype)


def _gla_branch(p1, pq, cos, sin, wdf, wdb, bdf, bdb, ng, bsz, ctx_len, slab):
    npair = slab // (2 * GLA_CHUNK)
    head = lambda b, h: (h, 0, 0)
    return pl.pallas_call(
        functools.partial(_gla_kernel, ctx_len=ctx_len),
        grid=(bsz, GLA_HEADS),
        in_specs=[pl.BlockSpec((slab, GLA_DKP), lambda b, h: (b, PQ_Q // GLA_DKP + h)),
                  pl.BlockSpec((slab, GLA_DKP), lambda b, h: (b, PQ_K // GLA_DKP + h)),
                  pl.BlockSpec((slab, GLA_DV), lambda b, h: (b, P_V // GLA_DV + h)),
                  pl.BlockSpec((slab, GLA_DV), lambda b, h: (b, P_G // GLA_DV + h)),
                  pl.BlockSpec((slab, 128), lambda b, h: (b, PQ_Q // 128)),
                  pl.BlockSpec((slab, GLA_DKP // 2), lambda b, h: (0, 0)),
                  pl.BlockSpec((slab, GLA_DKP // 2), lambda b, h: (0, 0)),
                  pl.BlockSpec((None, 128, GLA_DKP), head),
                  pl.BlockSpec((None, 128, GLA_DKP), head),
                  pl.BlockSpec((None, 1, GLA_DKP), head),
                  pl.BlockSpec((None, 1, GLA_DKP), head),
                  pl.BlockSpec((None, 1, GLA_DV), head)],
        out_specs=pl.BlockSpec((slab, GLA_DV), lambda b, h: (b, h)),
        out_shape=jax.ShapeDtypeStruct((bsz * slab, GLA_HEADS * GLA_DV), BF16),
        scratch_shapes=[pltpu.VMEM((2, slab, GLA_DKP), BF16),
                        pltpu.VMEM((2, slab, GLA_DKP), BF16),
                        pltpu.VMEM((2, slab, GLA_DKP), BF16),
                        pltpu.VMEM((2, slab, GLA_DKP), BF16),
                        pltpu.VMEM((2, npair, 1, GLA_DKP), F32),
                        pltpu.VMEM((2, slab, GLA_DV), F32),
                        pltpu.VMEM((2, GLA_DV, GLA_DKP), F32)],
        compiler_params=_cparams(2),
        name="gla_branch",
    )(pq, pq, p1, p1, pq, cos, sin, wdf, wdb, bdf, bdb, ng)


def _na_group_base(g, rows):
    return min(max(g * NA_GROUP - NA_KR // 2, 0), rows - NA_WIN)


def _na_build_bias(rpb_ref, bias_ref, rows):
    w = GRID_W
    qc = lax.broadcasted_iota(jnp.int32, (w, w), 0)
    kc = lax.broadcasted_iota(jnp.int32, (w, w), 1)
    win = jnp.clip(qc - NA_KC // 2, 0, w - NA_KC)
    col_ok = (kc >= win) & (kc < win + NA_KC)
    neg = jnp.full((w, w), -jnp.inf, F32)
    blocks = []
    for ro in range(2 * NA_KR - 1):
        row = jnp.broadcast_to(rpb_ref[ro:ro + 1, :], (w, 128))
        band = pltpu.roll(row, 128 - (NA_KC - 1), axis=1, stride=1, stride_axis=0)
        blocks.append(jnp.where(col_ok, band[:, :w] * LOG2E, neg))
    ngroups = rows // NA_GROUP
    for kind, g in enumerate((0, 1, ngroups - 1)):
        base = _na_group_base(g, rows)
        for qr in range(NA_GROUP):
            r_abs = g * NA_GROUP + qr
            win_r = min(max(r_abs - NA_KR // 2, 0), rows - NA_KR)
            pieces = []
            for kr in range(NA_WIN):
                key_r = base + kr
                inside = win_r <= key_r < win_r + NA_KR
                pieces.append(blocks[key_r - r_abs + NA_KR - 1] if inside else neg)
            bias_ref[kind, qr * w:(qr + 1) * w, :] = jnp.concatenate(pieces, axis=1)


def _na_kernel(q_ref, k_ref, v_ref, rpb_ref, o_ref, bias_ref, *, ctx_len, rows):
    scale = NA_HD ** -0.5 * LOG2E
    gq = NA_GROUP * GRID_W
    nwin = NA_WIN * GRID_W
    kc = k_ref[0:ctx_len, :]
    vc = v_ref[0:ctx_len, :]

    @pl.when(pl.program_id(1) == 0)
    def _():
        _na_build_bias(rpb_ref, bias_ref, rows)

    s = _dot_nt(q_ref[0:ctx_len, :], kc) * scale
    e = jnp.exp2(s - jnp.max(s, axis=-1, keepdims=True))
    pr = e / jnp.sum(e, axis=-1, keepdims=True)
    o_ref[0:ctx_len, :] = _dot(pr.astype(BF16), vc).astype(o_ref.dtype)

    ngroups = rows // NA_GROUP
    for g in range(ngroups):
        kind = 0 if g == 0 else (2 if g == ngroups - 1 else 1)
        q0 = ctx_len + g * gq
        k0 = ctx_len + _na_group_base(g, rows) * GRID_W
        qg = q_ref[q0:q0 + gq, :]
        s_loc = _dot_nt(qg, k_ref[k0:k0 + nwin, :]) * scale + bias_ref[kind]
        s_ctx = _dot_nt(qg, kc) * scale
        m = jnp.maximum(jnp.max(s_loc, axis=-1, keepdims=True), jnp.max(s_ctx, axis=-1, keepdims=True))
        e_loc = jnp.exp2(s_loc - m)
        e_ctx = jnp.exp2(s_ctx - m)
        den = jnp.sum(e_loc, axis=-1, keepdims=True) + jnp.sum(e_ctx, axis=-1, keepdims=True)
        o = _dot(e_loc.astype(BF16), v_ref[k0:k0 + nwin, :]) + _dot(e_ctx.astype(BF16), vc)
        o_ref[q0:q0 + gq, :] = (o / den).astype(o_ref.dtype)


def _na_branch(p, rpb, l, bsz, ctx_len, slab):
    rows = (slab - ctx_len) // GRID_W
    return pl.pallas_call(
        functools.partial(_na_kernel, ctx_len=ctx_len, rows=rows),
        grid=(NA_HEADS, bsz),
        in_specs=[pl.BlockSpec((slab, NA_HD), lambda h, b: (b, P_NQ // NA_HD + h)),
                  pl.BlockSpec((slab, NA_HD), lambda h, b: (b, P_NK // NA_HD + h)),
                  pl.BlockSpec((slab, NA_HD), lambda h, b: (b, P_NV // NA_HD + h)),
                  pl.BlockSpec((None, None) + rpb.shape[2:], lambda h, b: (l, h, 0, 0))],
        out_specs=pl.BlockSpec((slab, NA_HD), lambda h, b: (b, h)),
        out_shape=jax.ShapeDtypeStruct((bsz * slab, NA_W), BF16),
        scratch_shapes=[pltpu.VMEM((3, NA_GROUP * GRID_W, NA_WIN * GRID_W), F32)],
        compiler_params=_cparams(2),
        name="na_branch",
    )(p, p, p, rpb)


def _gla_dk_order():
    q = GLA_DK // 4
    one = np.full((GLA_DKP,), -1, np.int64)
    one[0:q] = np.arange(0, q)
    one[q:2 * q] = np.arange(2 * q, 3 * q)
    one[GLA_DKP // 2:GLA_DKP // 2 + q] = np.arange(q, 2 * q)
    one[GLA_DKP // 2 + q:GLA_DKP // 2 + 2 * q] = np.arange(3 * q, 4 * q)
    out = np.concatenate([np.where(one >= 0, one + h * GLA_DK, -1) for h in range(GLA_HEADS)])
    return out


def _take_cols(w, cols, axis=-1):
    axis = axis % w.ndim
    pieces = []
    i = 0
    n = len(cols)
    while i < n:
        j = i + 1
        if cols[i] < 0:
            while j < n and cols[j] < 0:
                j += 1
            pieces.append(jnp.zeros(w.shape[:axis] + (j - i,) + w.shape[axis + 1:], w.dtype))
        else:
            while j < n and cols[j] == cols[j - 1] + 1:
                j += 1
            pieces.append(lax.slice_in_dim(w, int(cols[i]), int(cols[i]) + (j - i), axis=axis))
        i = j
    return jnp.concatenate(pieces, axis=axis)


def _in_proj_offsets():
    o_q = 3 * CONV_W
    o_k = o_q + GLA_HEADS * GLA_DK
    o_v = o_k + GLA_HEADS * GLA_DK
    o_lr = o_v + 2 * GLA_HEADS * GLA_DV
    o_nq = o_lr + 2 * GLA_RANK
    return o_q, o_k, o_v, o_lr, o_nq


def _qk_weights(w_in_t):
    o_q, o_k, o_v, o_lr, o_nq = _in_proj_offsets()
    order = _gla_dk_order()
    q_order = np.where(order >= 0, order + o_q, -1)
    q_order[LR_LANE:LR_LANE + 2 * GLA_RANK] = o_lr + np.arange(2 * GLA_RANK)
    k_order = np.where(order >= 0, order + o_k, -1)
    return _take_cols(w_in_t, np.concatenate([q_order, k_order]), axis=1)


def _rope_tables(ctx_len, seq):
    q = GLA_DK // 4
    pos = jnp.arange(seq)
    freq = ROPE_THETA ** (-jnp.arange(q, dtype=F32) / q)
    ang_r = (pos // GRID_W).astype(F32)[:, None] * freq[None, :]
    ang_c = (pos % GRID_W).astype(F32)[:, None] * freq[None, :]
    pad = GLA_DKP // 2 - 2 * q
    cos = jnp.concatenate([jnp.cos(ang_r), jnp.cos(ang_c), jnp.ones((seq, pad), F32)], axis=1)
    sin = jnp.concatenate([jnp.sin(ang_r), jnp.sin(ang_c), jnp.zeros((seq, pad), F32)], axis=1)
    cos = jnp.concatenate([jnp.ones((ctx_len, GLA_DKP // 2), F32), cos], axis=0)
    sin = jnp.concatenate([jnp.zeros((ctx_len, GLA_DKP // 2), F32), sin], axis=0)
    return cos, sin


def kernel(x, c, ctx, c_ctx, mod_a, mod_b, mod_bias, norm_g, ffn_up, ffn_down, w_in, conv_w,
           gla_decay_w, gla_decay_b, gla_norm_g, na_rpb, w_branch_conv, w_branch_gla, w_branch_na,
           w_out, final_g):
    bsz, seq, d = x.shape
    ctx_len = ctx.shape[1]
    depth = mod_a.shape[0]
    slab = ctx_len + seq
    rows = seq // GRID_W
    assert slab % TM == 0 and d % TN == 0 and seq % ctx_len == 0
    assert rows % NA_GROUP == 0 and rows >= NA_WIN
    assert ctx_len % (2 * GLA_CHUNK) == 0 and seq % (2 * GLA_CHUNK) == 0

    mrows = 16
    cvec = jnp.concatenate([c_ctx[None, :], c, jnp.zeros((mrows - 1 - bsz, d), F32)], axis=0)
    mods = _modulation(cvec, mod_a, mod_b, mod_bias)[:, :1 + bsz].reshape(depth, 1 + bsz, 1, N_MOD * d)

    o_q, _, o_v, o_lr, o_nq = _in_proj_offsets()
    p_runs = [(0, o_q), (o_v, o_lr - o_v), (o_nq, w_in.shape[-1] - o_nq)]
    w_in_t = jnp.swapaxes(w_in, 1, 2)
    wq_t = _qk_weights(w_in_t)
    ffn_up_b = ffn_up.astype(BF16)
    ffn_down_b = ffn_down.astype(BF16).reshape(depth * 2, ffn_down.shape[2], d)
    wb_conv = w_branch_conv.astype(BF16)
    wb_gla = w_branch_gla.astype(BF16)
    wb_na = w_branch_na.astype(BF16)
    w_out_b = w_out.astype(BF16)
    order = _gla_dk_order()
    dw = _take_cols(gla_decay_w, order).reshape(depth, 2, GLA_RANK, GLA_HEADS, GLA_DKP)
    dw = dw.transpose(0, 1, 3, 2, 4)
    wdf = jnp.zeros((depth, GLA_HEADS, 128, GLA_DKP), F32).at[:, :, LR_LANE:LR_LANE + GLA_RANK].set(dw[:, 0])
    wdb = jnp.zeros((depth, GLA_HEADS, 128, GLA_DKP), F32).at[:, :, LR_LANE + GLA_RANK:LR_LANE + 2 * GLA_RANK].set(dw[:, 1])
    wdf = wdf.astype(BF16)
    wdb = wdb.astype(BF16)
    db = _take_cols(gla_decay_b, order).reshape(depth, 2, GLA_HEADS, 1, GLA_DKP)
    ng = gla_norm_g.reshape(depth, GLA_HEADS, 1, GLA_DV)
    cos, sin = _rope_tables(ctx_len, seq)
    rpb_pad = jnp.pad(na_rpb, ((0, 0), (0, 0), (0, 1), (0, 128 - na_rpb.shape[-1])))

    h = jnp.concatenate([ctx, x], axis=1).reshape(bsz * slab, d)
    for l in range(depth):
        m = mods[l]
        nm = functools.partial(_norm_mod, m=m, ctx_len=ctx_len, slab=slab)
        rp = functools.partial(_resid_proj, m=m, ctx_len=ctx_len, slab=slab)
        act = _ffn_up(nm(h, norm_g[l, 0:1], sub=0), ffn_up_b, l, 0)
        h = rp(act, ffn_down_b, 2 * l, h, sub=0, scale=0.5)
        u = nm(h, norm_g[l, 1:2], sub=1)
        p = _proj_runs(u, w_in_t, l, p_runs)
        pq = _proj_runs(u, wq_t, l, [(0, wq_t.shape[1])])
        ya = _conv_branch(p, conv_w[l], bsz, ctx_len, slab)
        yb = _gla_branch(p, pq, cos, sin, wdf[l], wdb[l], db[l, 0], db[l, 1], ng[l], bsz, ctx_len, slab)
        yc = _na_branch(p, rpb_pad, l, bsz, ctx_len, slab)
        z = _branch_proj(ya, yb, yc, wb_conv, wb_gla, wb_na, l, p)
        h = rp(z, w_out_b, l, h, sub=1, scale=1.0)
        act = _ffn_up(nm(h, norm_g[l, 2:3], sub=2), ffn_up_b, l, 1)
        h = rp(act, ffn_down_b, 2 * l + 1, h, sub=2, scale=0.5)
    return _final_norm(h, final_g.reshape(1, d), bsz, ctx_len, seq)
```
